```python
import jax, jax.numpy as jnp
from jax import lax
import numpy as np

D_MODEL = 1024
BATCH = 16
SEQ = 2048
DEPTH = 1

LRU_WIDTH = D_MODEL
LRU_BLOCKS = 8
LRU_BLOCK_DIM = LRU_WIDTH // LRU_BLOCKS
CONV_WIDTH = 4
LRU_C = 8.0
HG_WIDTH = D_MODEL
HG_EXPAND = 128
HG_HEADS = HG_WIDTH // HG_EXPAND
HG_HEAD_V = HG_WIDTH // HG_HEADS
CHUNK = 64
HG_SCALE = HG_EXPAND ** -0.5
N_BRANCH = 2
EPS = 1e-6
IN_COLS = 2 * LRU_WIDTH + 4 * HG_WIDTH + N_BRANCH * D_MODEL
SPLITS = [LRU_WIDTH, 2 * LRU_WIDTH, 2 * LRU_WIDTH + HG_WIDTH, 2 * LRU_WIDTH + 2 * HG_WIDTH,
          2 * LRU_WIDTH + 3 * HG_WIDTH, 2 * LRU_WIDTH + 4 * HG_WIDTH]

kernel_name = "hybrid_hawk_hgrn2_gated_block"


def rms_norm(x, g):
    xf = x.astype(jnp.float32)
    y = xf * lax.rsqrt(jnp.mean(xf * xf, axis=-1, keepdims=True) + EPS)
    return y.astype(x.dtype) * g


def causal_depthwise_conv(x, w, b):
    s = x.shape[1]
    xp = jnp.pad(x, ((0, 0), (CONV_WIDTH - 1, 0), (0, 0)))
    return b + sum(xp[:, k:k + s] * w[k] for k in range(CONV_WIDTH))


def block_diag_linear(x, w, b):
    bsz, s, _ = x.shape
    xb = x.reshape(bsz, s, LRU_BLOCKS, LRU_BLOCK_DIM)
    y = jnp.einsum('bshi,hij->bshj', xb, w) + b
    return y.reshape(bsz, s, LRU_WIDTH)


def _linear_recurrence_combine(left, right):
    a_l, u_l = left
    a_r, u_r = right
    return a_l * a_r, a_r * u_l + u_r


def rg_lru(x, wx, bx, wa, ba, lam):
    xf = x.astype(jnp.float32)
    gate_i = jax.nn.sigmoid(block_diag_linear(xf, wx, bx))
    gate_r = jax.nn.sigmoid(block_diag_linear(xf, wa, ba))
    log_a = -LRU_C * gate_r * jax.nn.softplus(-lam.astype(jnp.float32))
    a = jnp.exp(log_a)
    mult = jnp.sqrt(-jnp.expm1(2.0 * log_a))
    u = mult * gate_i * xf
    _, h = lax.associative_scan(_linear_recurrence_combine, (a, u), axis=1)
    return h.astype(x.dtype)


def hgrn2_chunked(q, k, v, log_f):
    bsz, s, h, dk = q.shape
    dv = v.shape[-1]
    n = s // CHUNK

    def to_chunks(t):
        return t.reshape(bsz, n, CHUNK, h, t.shape[-1]).transpose(1, 0, 3, 2, 4)

    q, k, v, log_f = map(to_chunks, (q, k, v, log_f))
    b = jnp.cumsum(log_f, axis=3)
    b_mid = b[:, :, :, CHUNK // 2:CHUNK // 2 + 1]
    b_last = b[:, :, :, -1:]
    q_in = q * jnp.exp(b - b_mid) * HG_SCALE
    k_in = k * jnp.exp(b_mid - b)
    causal = jnp.tril(jnp.ones((CHUNK, CHUNK), dtype=bool))
    att = jnp.where(causal, jnp.einsum('nbhtd,nbhsd->nbhts', q_in, k_in), 0.0)
    o_intra = jnp.einsum('nbhts,nbhsv->nbhtv', att, v)
    q_inter = q * jnp.exp(b) * HG_SCALE
    k_state = k * jnp.exp(b_last - b)
    chunk_decay = jnp.exp(b_last[:, :, :, 0])

    def step(state, inp):
        qc, kc, vc, dc = inp
        o = jnp.einsum('bhtd,bhdv->bhtv', qc, state)
        state = state * dc[..., None] + jnp.einsum('bhsd,bhsv->bhdv', kc, vc)
        return state, o

    s0 = jnp.zeros((bsz, h, dk, dv), dtype=q.dtype)
    _, o_inter = lax.scan(step, s0, (q_inter, k_state, v, chunk_decay))
    o = o_intra + o_inter
    return o.transpose(1, 0, 3, 2, 4).reshape(bsz, s, h, dv)


def setup_inputs(seed: int = 0) -> dict:
    key = jax.random.key(seed)
    ks = jax.random.split(key, 20)
    f32 = jnp.float32
    nrm = lambda k, shape, scale: jax.random.normal(k, shape, f32) * scale
    x = jax.random.normal(ks[0], (BATCH, SEQ, D_MODEL), f32)
    w_in = nrm(ks[1], (DEPTH, D_MODEL, IN_COLS), D_MODEL ** -0.5)
    b_merge = nrm(ks[2], (DEPTH, N_BRANCH * D_MODEL), 0.01)
    conv_w = nrm(ks[3], (DEPTH, CONV_WIDTH, LRU_WIDTH), CONV_WIDTH ** -0.5)
    conv_b = nrm(ks[4], (DEPTH, LRU_WIDTH), 0.01)
    rg_wx = nrm(ks[5], (DEPTH, LRU_BLOCKS, LRU_BLOCK_DIM, LRU_BLOCK_DIM), LRU_BLOCK_DIM ** -0.5)
    rg_bx = nrm(ks[6], (DEPTH, LRU_BLOCKS, LRU_BLOCK_DIM), 0.01)
    rg_wa = nrm(ks[7], (DEPTH, LRU_BLOCKS, LRU_BLOCK_DIM, LRU_BLOCK_DIM), LRU_BLOCK_DIM ** -0.5)
    rg_ba = nrm(ks[8], (DEPTH, LRU_BLOCKS, LRU_BLOCK_DIM), 0.01)
    u = jax.random.uniform(ks[9], (DEPTH, LRU_WIDTH), f32, minval=0.9, maxval=0.999)
    a0 = u ** (1.0 / LRU_C)
    rg_lambda = jnp.log(a0) - jnp.log1p(-a0)
    hg_lb_logits = nrm(ks[10], (DEPTH + 1, HG_WIDTH), 0.1)
    hg_norm_g = 1.0 + nrm(ks[11], (DEPTH, HG_HEAD_V), 0.05)
    proj_a = nrm(ks[12], (DEPTH, LRU_WIDTH, D_MODEL), LRU_WIDTH ** -0.5)
    proj_b = nrm(ks[13], (DEPTH, HG_WIDTH, D_MODEL), HG_WIDTH ** -0.5)
    w_out = nrm(ks[14], (DEPTH, D_MODEL, D_MODEL), D_MODEL ** -0.5)
    norm_g = 1.0 + nrm(ks[15], (DEPTH, D_MODEL), 0.05)
    final_norm_g = 1.0 + nrm(ks[16], (D_MODEL,), 0.05)
    return {"x": x, "w_in": w_in, "b_merge": b_merge, "conv_w": conv_w, "conv_b": conv_b,
            "rg_wx": rg_wx, "rg_bx": rg_bx, "rg_wa": rg_wa, "rg_ba": rg_ba, "rg_lambda": rg_lambda,
            "hg_lb_logits": hg_lb_logits, "hg_norm_g": hg_norm_g, "proj_a": proj_a, "proj_b": proj_b,
            "w_out": w_out, "norm_g": norm_g, "final_norm_g": final_norm_g}


def reference(x, w_in, b_merge, conv_w, conv_b, rg_wx, rg_bx, rg_wa, rg_ba, rg_lambda,
              hg_lb_logits, hg_norm_g, proj_a, proj_b, w_out, norm_g, final_norm_g):
    bsz, s, _ = x.shape
    lb_all = jnp.cumsum(jax.nn.softmax(hg_lb_logits.astype(jnp.float32), axis=0), axis=0)
    for l in range(DEPTH):
        h = rms_norm(x, norm_g[l])
        z = h @ w_in[l]
        xa, ga, q, f_pre, i_in, gb, gm = jnp.split(z, SPLITS, axis=-1)

        xa = causal_depthwise_conv(xa, conv_w[l], conv_b[l])
        ya = rg_lru(xa, rg_wx[l], rg_bx[l], rg_wa[l], rg_ba[l], rg_lambda[l])
        ya = ya * jax.nn.silu(ga)
        out_a = ya @ proj_a[l]

        lb = lb_all[l]
        f = lb + (1.0 - lb) * jax.nn.sigmoid(f_pre.astype(jnp.float32))
        log_f = jnp.log(f)
        k = 1.0 - f
        qh = jax.nn.silu(q.astype(jnp.float32))
        heads = lambda t: t.reshape(bsz, s, HG_HEADS, t.shape[-1] // HG_HEADS)
        o = hgrn2_chunked(heads(qh), heads(k), heads(i_in.astype(jnp.float32)), heads(log_f))
        o = rms_norm(o, hg_norm_g[l]).reshape(bsz, s, HG_WIDTH).astype(x.dtype)
        yb = o * jax.nn.silu(gb)
        out_b = yb @ proj_b[l]

        gates = jax.nn.sigmoid(gm + b_merge[l])
        g_a, g_b = jnp.split(gates, N_BRANCH, axis=-1)
        mixed = g_a * out_a + g_b * out_b
        x = x + mixed @ w_out[l]
    return rms_norm(x, final_norm_g)
```

```python
import functools

import jax
import jax.numpy as jnp
from jax import lax
from jax.experimental import pallas as pl
from jax.experimental.pallas import tpu as pltpu

D_MODEL = 1024
LRU_BLOCK_DIM = 128
CONV_WIDTH = 4
LRU_C = 8.0
HEAD_DIM = 128
CHUNK = 64
HG_SCALE = HEAD_DIM ** -0.5
EPS = 1e-6

NB = 8
TT = CHUNK
M = NB * TT
CB = 256
NCB = D_MODEL // CB
HPB = CB // HEAD_DIM
LANES = 128
LT = CB // LANES
PITCH = TT + 8
VMEM_LIMIT_BYTES = 56 * 1024 * 1024

G_XA, G_GA, G_Q, G_F, G_I, G_GB, G_MA, G_MB = range(8)

_f32 = jnp.float32
_bf16 = jnp.bfloat16


def _dot(a, b):
    return jnp.dot(a, b, preferred_element_type=_f32)


def _sigmoid(x):
    return 1.0 / (1.0 + jnp.exp(-x))


def _silu(x):
    return x * _sigmoid(x)


def _to_pitched(scr, val):
    for l in range(LT):
        for b in range(NB):
            scr[l, b * PITCH:b * PITCH + TT, :] = val[b * TT:(b + 1) * TT, l * LANES:(l + 1) * LANES]


def _from_pitched(scr, b):
    return jnp.concatenate([scr[l, b * PITCH:b * PITCH + TT, :] for l in range(LT)], axis=1)


def _block_kernel(x_ref, w_in_ref, wbd_ref, pa_ref, pb_ref, wo_ref,
                  cw_ref, cvec_ref, lbl_ref, bm_ref, ng_ref, fg_ref,
                  o_ref,
                  hb_scr, xc_scr, a_scr, u_scr, h_scr, lf_scr, hcar_scr, st_scr,
                  ya_scr, yb_scr, mix_scr, res_scr):
    j = pl.program_id(1)

    @pl.when(j == 0)
    def _reset_carries():
        xc_scr[:, :, 0:8, :] = jnp.zeros((NCB, NB, 8, CB), _f32)
        hcar_scr[...] = jnp.zeros_like(hcar_scr)
        st_scr[...] = jnp.zeros_like(st_scr)

    ng = ng_ref[...]
    for b in range(NB):
        xb = x_ref[b]
        ms = jnp.mean(xb * xb, axis=-1, keepdims=True)
        hb_scr[b * TT:(b + 1) * TT, :] = ((xb * lax.rsqrt(ms + EPS)) * ng).astype(_bf16)

    causal = (lax.broadcasted_iota(jnp.int32, (TT, TT), 0)
              >= lax.broadcasted_iota(jnp.int32, (TT, TT), 1))

    def channel_block(c, carry):
        hb = hb_scr[...]
        cvec = cvec_ref[c]
        conv_b, bx, ba, lam, hg_g = (cvec[k:k + 1] for k in range(5))
        cw = cw_ref[c]

        xa_pre = _dot(hb, w_in_ref[G_XA * NCB + c])
        for b in range(NB):
            xc_scr[c, b, 8:8 + TT, :] = xa_pre[b * TT:(b + 1) * TT]
        xa_rows = []
        for b in range(NB):
            acc = conv_b + cw[3:4] * xc_scr[c, b, 8:8 + TT, :]
            for k in range(CONV_WIDTH - 1):
                acc = acc + cw[k:k + 1] * xc_scr[c, b, 5 + k:5 + k + TT, :]
            xa_rows.append(acc)
            xc_scr[c, b, 5:8, :] = xc_scr[c, b, TT + 5:TT + 8, :]
        xa = jnp.concatenate(xa_rows, axis=0)

        gates = _dot(xa.astype(_bf16), wbd_ref[c])
        gate_i = _sigmoid(gates[:, :CB] + bx)
        gate_r = _sigmoid(gates[:, CB:] + ba)
        neg_lam = -lam
        softplus = jnp.maximum(neg_lam, 0.0) + jnp.log(1.0 + jnp.exp(-jnp.abs(neg_lam)))
        log_a = (-LRU_C) * gate_r * softplus
        a = jnp.exp(log_a)
        mult = jnp.sqrt(1.0 - a * a)
        u = mult * gate_i * xa
        _to_pitched(a_scr, a)
        _to_pitched(u_scr, u)

        q = _silu(_dot(hb, w_in_ref[G_Q * NCB + c]))
        f_pre = _dot(hb, w_in_ref[G_F * NCB + c])
        lbl = lbl_ref[c]
        lmax = jnp.maximum(lbl[0:1], lbl[1:2])
        e0 = jnp.exp(lbl[0:1] - lmax)
        e1 = jnp.exp(lbl[1:2] - lmax)
        lb = e0 / (e0 + e1)
        f = lb + (1.0 - lb) * _sigmoid(f_pre)
        log_f = jnp.log(f)
        kk = 1.0 - f
        _to_pitched(lf_scr, log_f)

        def scan_step(t, hc):
            rows = pl.ds(t, NB, stride=PITCH)
            out = []
            for l in range(LT):
                h, cum = hc[l]
                h = a_scr[l, rows, :] * h + u_scr[l, rows, :]
                cum = cum + lf_scr[l, rows, :]
                h_scr[l, rows, :] = h
                lf_scr[l, rows, :] = cum
                out.append((h, cum))
            return tuple(out)

        h0 = hcar_scr[c]
        zero = jnp.zeros((NB, LANES), _f32)
        hc = lax.fori_loop(
            0, TT, scan_step,
            tuple((h0[:, l * LANES:(l + 1) * LANES], zero) for l in range(LT)), unroll=8)
        hcar_scr[c] = jnp.concatenate([hc[l][0] for l in range(LT)], axis=1)

        ga = _dot(hb, w_in_ref[G_GA * NCB + c])
        h_all = jnp.concatenate([_from_pitched(h_scr, b) for b in range(NB)], axis=0)
        ya_scr[c] = (h_all * _silu(ga)).astype(_bf16)

        v = _dot(hb, w_in_ref[G_I * NCB + c])
        gb = _dot(hb, w_in_ref[G_GB * NCB + c])
        for b in range(NB):
            rs = slice(b * TT, (b + 1) * TT)
            bc = _from_pitched(lf_scr, b)
            b_mid = bc[CHUNK // 2:CHUNK // 2 + 1]
            b_last = bc[CHUNK - 1:CHUNK]
            q_in = q[rs] * jnp.exp(bc - b_mid) * HG_SCALE
            k_in = kk[rs] * jnp.exp(b_mid - bc)
            q_inter = (q_in * jnp.exp(b_mid)).astype(_bf16)
            k_state = (k_in * jnp.exp(b_last - b_mid)).astype(_bf16)
            decay = jnp.exp(b_last)
            q_in = q_in.astype(_bf16)
            k_in = k_in.astype(_bf16)
            vb = v[rs].astype(_bf16)
            o_heads = []
            for hh in range(HPB):
                ls = slice(hh * HEAD_DIM, (hh + 1) * HEAD_DIM)
                att = lax.dot_general(q_in[:, ls], k_in[:, ls], (((1,), (1,)), ((), ())),
                                      preferred_element_type=_f32)
                att = jnp.where(causal, att, 0.0).astype(_bf16)
                st = st_scr[c, b, hh]
                o_h = _dot(att, vb[:, ls]) + lax.dot_general(
                    q_inter[:, ls], st.astype(_bf16), (((1,), (1,)), ((), ())),
                    preferred_element_type=_f32)
                st_scr[c, b, hh] = st * decay[:, ls] + lax.dot_general(
                    vb[:, ls], k_state[:, ls], (((0,), (0,)), ((), ())),
                    preferred_element_type=_f32)
                ms = jnp.mean(o_h * o_h, axis=-1, keepdims=True)
                o_heads.append(o_h * lax.rsqrt(ms + EPS))
            o_n = jnp.concatenate(o_heads, axis=1) * hg_g
            yb_scr[c, b * TT:(b + 1) * TT, :] = (o_n * _silu(gb[rs])).astype(_bf16)
        return carry

    lax.fori_loop(0, NCB, channel_block, 0)

    def merge_block(n, carry):
        hb = hb_scr[...]
        out_a = _dot(ya_scr[0], pa_ref[n])
        out_b = _dot(yb_scr[0], pb_ref[n])
        for c in range(1, NCB):
            out_a = out_a + _dot(ya_scr[c], pa_ref[c * NCB + n])
            out_b = out_b + _dot(yb_scr[c], pb_ref[c * NCB + n])
        bm = bm_ref[n]
        g_a = _sigmoid(_dot(hb, w_in_ref[G_MA * NCB + n]) + bm[0:1])
        g_b = _sigmoid(_dot(hb, w_in_ref[G_MB * NCB + n]) + bm[1:2])
        mix_scr[n] = (g_a * out_a + g_b * out_b).astype(_bf16)
        return carry

    lax.fori_loop(0, NCB, merge_block, 0)

    ssq = jnp.zeros((M, 1), _f32)
    for n in range(NCB):
        ls = slice(n * CB, (n + 1) * CB)
        res = x_ref[:, :, ls].reshape(M, CB)
        for c in range(NCB):
            res = res + _dot(mix_scr[c], wo_ref[c * NCB + n])
        res_scr[:, ls] = res
        ssq = ssq + jnp.sum(res * res, axis=-1, keepdims=True)
    rinv = lax.rsqrt(ssq * (1.0 / D_MODEL) + EPS)
    fg = fg_ref[...]
    for b in range(NB):
        rs = slice(b * TT, (b + 1) * TT)
        o_ref[b] = res_scr[rs, :] * rinv[rs] * fg


def _col_blocks(w):
    k, n = w.shape
    return w.reshape(k, n // CB, CB).transpose(1, 0, 2)


def _tile_blocks(w):
    k, n = w.shape
    return w.reshape(k // CB, CB, n // CB, CB).transpose(0, 2, 1, 3).reshape(-1, CB, CB)


def _vec_blocks(rows):
    return jnp.stack(rows, axis=0).reshape(len(rows), NCB, CB).transpose(1, 0, 2)


def _resident(shape):
    zeros = (0,) * len(shape)
    return pl.BlockSpec(shape, lambda i, j: zeros, pipeline_mode=pl.Buffered(1))


@jax.jit
def kernel(x, w_in, b_merge, conv_w, conv_b, rg_wx, rg_bx, rg_wa, rg_ba, rg_lambda,
           hg_lb_logits, hg_norm_g, proj_a, proj_b, w_out, norm_g, final_norm_g):
    bsz, seq, d = x.shape
    assert d == D_MODEL and bsz % NB == 0 and seq % TT == 0
    assert w_in.shape[0] == 1, "single-layer block"

    w_in_b = _col_blocks(w_in[0].astype(_bf16))
    lpb = CB // LRU_BLOCK_DIM
    wbd = jnp.zeros((NCB, CB, 2 * CB), _f32)
    for c in range(NCB):
        for k in range(lpb):
            r = slice(k * LRU_BLOCK_DIM, (k + 1) * LRU_BLOCK_DIM)
            wbd = wbd.at[c, r, r].set(rg_wx[0, c * lpb + k])
            wbd = wbd.at[c, r, CB + k * LRU_BLOCK_DIM:CB + (k + 1) * LRU_BLOCK_DIM].set(
                rg_wa[0, c * lpb + k])
    wbd = wbd.astype(_bf16)
    pa = _tile_blocks(proj_a[0].astype(_bf16))
    pb = _tile_blocks(proj_b[0].astype(_bf16))
    wo = _tile_blocks(w_out[0].astype(_bf16))

    zeros = jnp.zeros((D_MODEL,), _f32)
    cvec = _vec_blocks([conv_b[0], rg_bx[0].reshape(-1), rg_ba[0].reshape(-1), rg_lambda[0],
                        jnp.tile(hg_norm_g[0], D_MODEL // HEAD_DIM), zeros, zeros, zeros])
    cw = _vec_blocks([conv_w[0, k] for k in range(CONV_WIDTH)])
    lbl = _vec_blocks([hg_lb_logits[0], hg_lb_logits[1]])
    bm = _vec_blocks([b_merge[0, :D_MODEL], b_merge[0, D_MODEL:]])
    ng = norm_g[0].reshape(1, D_MODEL)
    fg = final_norm_g.reshape(1, D_MODEL)

    x_spec = pl.BlockSpec((NB, TT, D_MODEL), lambda i, j: (i, j, 0))
    operands = (w_in_b, wbd, pa, pb, wo, cw, cvec, lbl, bm, ng, fg)
    return pl.pallas_call(
        _block_kernel,
        grid=(bsz // NB, seq // TT),
        in_specs=[x_spec] + [_resident(op.shape) for op in operands],
        out_specs=x_spec,
        out_shape=jax.ShapeDtypeStruct(x.shape, x.dtype),
        scratch_shapes=[
            pltpu.VMEM((M, D_MODEL), _bf16),
            pltpu.VMEM((NCB, NB, PITCH, CB), _f32),
            pltpu.VMEM((LT, NB * PITCH, LANES), _f32),
            pltpu.VMEM((LT, NB * PITCH, LANES), _f32),
            pltpu.VMEM((LT, NB * PITCH, LANES), _f32),
            pltpu.VMEM((LT, NB * PITCH, LANES), _f32),
            pltpu.VMEM((NCB, NB, CB), _f32),
            pltpu.VMEM((NCB, NB, HPB, HEAD_DIM, HEAD_DIM), _f32),
            pltpu.VMEM((NCB, M, CB), _bf16),
            pltpu.VMEM((NCB, M, CB), _bf16),
            pltpu.VMEM((NCB, M, CB), _bf16),
            pltpu.VMEM((M, D_MODEL), _f32),
        ],
        compiler_params=pltpu.CompilerParams(
            dimension_semantics=("arbitrary", "arbitrary"),
            vmem_limit_bytes=VMEM_LIMIT_BYTES),
        name="hawk_hgrn2_block",
    )(x, *operands)
```

```python
import jax
import jax.numpy as jnp
from jax import lax
from jax.experimental import pallas as pl
from jax.experimental.pallas import tpu as pltpu

D_MODEL = 1024
LRU_BLOCK_DIM = 128
CONV_WIDTH = 4
LRU_C = 8.0
HEAD_DIM = 128
CHUNK = 64
HG_SCALE = HEAD_DIM ** -0.5
EPS = 1e-6

NB = 8
TT = CHUNK
M = NB * TT
CB = 256
NCB = D_MODEL // CB
HPB = CB // HEAD_DIM
LANES = 128
LT = CB // LANES
PITCH = TT + 8
VMEM_LIMIT_BYTES = 58 * 1024 * 1024

Z_XA, Z_GA, Z_Q, Z_F, Z_I, Z_GB = range(6)
NZ = 6

_f32 = jnp.float32
_bf16 = jnp.bfloat16


def _dot(a, b):
    return jnp.dot(a, b, preferred_element_type=_f32)


def _dot_nt(a, b):
    return lax.dot_general(a, b, (((1,), (1,)), ((), ())), preferred_element_type=_f32)


def _dot_tn(a, b):
    return lax.dot_general(a, b, (((0,), (0,)), ((), ())), preferred_element_type=_f32)


def _sigmoid(x):
    return 1.0 / (1.0 + jnp.exp(-x))


def _silu(x):
    return x * _sigmoid(x)


def _rows(b):
    return slice(b * TT, (b + 1) * TT)


def _to_pitched(scr, b, val):
    for l in range(LT):
        scr[l, b * PITCH:b * PITCH + TT, :] = val[:, l * LANES:(l + 1) * LANES]


def _from_pitched(scr, b):
    return jnp.concatenate([scr[l, b * PITCH:b * PITCH + TT, :] for l in range(LT)], axis=1)


def _block_kernel(x_ref, wz_ref, wm_ref, wbd_ref, pa_ref, pb_ref, wo_ref,
                  cw_ref, cvec_ref, lbl_ref, bm_ref, ng_ref, fg_ref,
                  o_ref,
                  hb_scr, z0_scr, z1_scr, xc_scr, tail_scr, a_scr, u_scr, h_scr, lf_scr,
                  hcar_scr, st_scr, ya_scr, yb_scr, mix_scr):
    j = pl.program_id(1)

    @pl.when(j == 0)
    def _reset_carries():
        tail_scr[...] = jnp.zeros_like(tail_scr)
        hcar_scr[...] = jnp.zeros_like(hcar_scr)
        st_scr[...] = jnp.zeros_like(st_scr)

    ng = ng_ref[...]
    for b in range(NB):
        xb = x_ref[b]
        ms = jnp.mean(xb * xb, axis=-1, keepdims=True)
        hb_scr[_rows(b), :] = ((xb * lax.rsqrt(ms + EPS)) * ng).astype(_bf16)

    causal = (lax.broadcasted_iota(jnp.int32, (TT, TT), 0)
              >= lax.broadcasted_iota(jnp.int32, (TT, TT), 1))
    z_bufs = (z0_scr, z1_scr)

    def project(c):
        z_bufs[c % 2][...] = _dot(hb_scr[...], wz_ref[c])

    def channel_block(c):
        z_scr = z_bufs[c % 2]
        cs = slice(c * CB, (c + 1) * CB)

        def zcols(g, b):
            return z_scr[_rows(b), g * CB:(g + 1) * CB]

        conv_b, bx, ba, lam, hg_g = (cvec_ref[k:k + 1, cs] for k in range(5))
        cw = cw_ref[:, cs]

        xa = []
        for b in range(NB):
            xc_scr[b, 0:8, :] = tail_scr[c, b]
            xc_scr[b, 8:8 + TT, :] = zcols(Z_XA, b)
            acc = conv_b + cw[3:4] * xc_scr[b, 8:8 + TT, :]
            for k in range(CONV_WIDTH - 1):
                acc = acc + cw[k:k + 1] * xc_scr[b, 5 + k:5 + k + TT, :]
            xa.append(acc)
            tail_scr[c, b] = xc_scr[b, TT:TT + 8, :]

        gates = _dot(jnp.concatenate(xa, axis=0).astype(_bf16), wbd_ref[c])
        neg_lam = -lam
        softplus = jnp.maximum(neg_lam, 0.0) + jnp.log(1.0 + jnp.exp(-jnp.abs(neg_lam)))
        rate = (-LRU_C) * softplus
        for b in range(NB):
            gate_i = _sigmoid(gates[_rows(b), :CB] + bx)
            gate_r = _sigmoid(gates[_rows(b), CB:] + ba)
            a = jnp.exp(rate * gate_r)
            mult = jnp.sqrt(1.0 - a * a)
            _to_pitched(a_scr, b, a)
            _to_pitched(u_scr, b, mult * gate_i * xa[b])

        lbl = lbl_ref[:, cs]
        lmax = jnp.maximum(lbl[0:1], lbl[1:2])
        e0 = jnp.exp(lbl[0:1] - lmax)
        e1 = jnp.exp(lbl[1:2] - lmax)
        lb = e0 / (e0 + e1)
        kk = []
        for b in range(NB):
            f = lb + (1.0 - lb) * _sigmoid(zcols(Z_F, b))
            _to_pitched(lf_scr, b, jnp.log(f))
            kk.append(1.0 - f)

        h0 = hcar_scr[c]
        hs = [h0[:, l * LANES:(l + 1) * LANES] for l in range(LT)]
        cums = [jnp.zeros((NB, LANES), _f32) for _ in range(LT)]
        for t in range(TT):
            rows = pl.ds(t, NB, stride=PITCH)
            for l in range(LT):
                hs[l] = a_scr[l, rows, :] * hs[l] + u_scr[l, rows, :]
                cums[l] = cums[l] + lf_scr[l, rows, :]
                h_scr[l, rows, :] = hs[l]
                lf_scr[l, rows, :] = cums[l]
        hcar_scr[c] = jnp.concatenate(hs, axis=1)

        for b in range(NB):
            ya_scr[_rows(b), cs] = (_from_pitched(h_scr, b) * _silu(zcols(Z_GA, b))).astype(_bf16)

            bc = _from_pitched(lf_scr, b)
            b_mid = bc[CHUNK // 2:CHUNK // 2 + 1]
            b_last = bc[CHUNK - 1:CHUNK]
            q_in = _silu(zcols(Z_Q, b)) * jnp.exp(bc - b_mid) * HG_SCALE
            k_in = kk[b] * jnp.exp(b_mid - bc)
            q_inter = (q_in * jnp.exp(b_mid)).astype(_bf16)
            k_state = (k_in * jnp.exp(b_last - b_mid)).astype(_bf16)
            decay = jnp.exp(b_last)
            q_in = q_in.astype(_bf16)
            k_in = k_in.astype(_bf16)
            vb = zcols(Z_I, b).astype(_bf16)
            o_heads = []
            for hh in range(HPB):
                ls = slice(hh * HEAD_DIM, (hh + 1) * HEAD_DIM)
                att = jnp.where(causal, _dot_nt(q_in[:, ls], k_in[:, ls]), 0.0).astype(_bf16)
                st = st_scr[c, b, hh]
                o_h = _dot(att, vb[:, ls]) + _dot_nt(q_inter[:, ls], st.astype(_bf16))
                st_scr[c, b, hh] = st * decay[:, ls] + _dot_tn(vb[:, ls], k_state[:, ls])
                ms = jnp.mean(o_h * o_h, axis=-1, keepdims=True)
                o_heads.append(o_h * lax.rsqrt(ms + EPS))
            o_n = jnp.concatenate(o_heads, axis=1) * hg_g
            yb_scr[_rows(b), cs] = (o_n * _silu(zcols(Z_GB, b))).astype(_bf16)

    project(0)
    for c in range(NCB):
        if c + 1 < NCB:
            project(c + 1)
        channel_block(c)

    for n in range(NCB):
        cs = slice(n * CB, (n + 1) * CB)
        gm = _dot(hb_scr[...], wm_ref[n])
        out_a = _dot(ya_scr[...], pa_ref[n])
        out_b = _dot(yb_scr[...], pb_ref[n])
        g_a = _sigmoid(gm[:, :CB] + bm_ref[0:1, cs])
        g_b = _sigmoid(gm[:, CB:] + bm_ref[1:2, cs])
        mix_scr[:, cs] = (g_a * out_a + g_b * out_b).astype(_bf16)

    ssq = [jnp.zeros((TT, 1), _f32) for _ in range(NB)]
    for n in range(NCB):
        cs = slice(n * CB, (n + 1) * CB)
        proj = _dot(mix_scr[...], wo_ref[n])
        for b in range(NB):
            res = x_ref[b, :, cs] + proj[_rows(b)]
            o_ref[b, :, cs] = res
            ssq[b] = ssq[b] + jnp.sum(res * res, axis=-1, keepdims=True)
    fg = fg_ref[...]
    for b in range(NB):
        o_ref[b] = o_ref[b] * lax.rsqrt(ssq[b] * (1.0 / D_MODEL) + EPS) * fg


def _col_blocks(w):
    k, n = w.shape
    return w.reshape(k, n // CB, CB).transpose(1, 0, 2)


def _resident(shape):
    zeros = (0,) * len(shape)
    return pl.BlockSpec(shape, lambda i, j: zeros, pipeline_mode=pl.Buffered(1))


@jax.jit
def kernel(x, w_in, b_merge, conv_w, conv_b, rg_wx, rg_bx, rg_wa, rg_ba, rg_lambda,
           hg_lb_logits, hg_norm_g, proj_a, proj_b, w_out, norm_g, final_norm_g):
    bsz, seq, d = x.shape
    assert d == D_MODEL and bsz % NB == 0 and seq % TT == 0
    assert w_in.shape[0] == 1, "single-layer block"

    w_in_b = w_in[0].astype(_bf16).reshape(D_MODEL, 8, NCB, CB)
    wz = w_in_b[:, :NZ].transpose(2, 0, 1, 3).reshape(NCB, D_MODEL, NZ * CB)
    wm = w_in_b[:, NZ:].transpose(2, 0, 1, 3).reshape(NCB, D_MODEL, 2 * CB)
    lpb = CB // LRU_BLOCK_DIM
    wbd = jnp.zeros((NCB, CB, 2 * CB), _f32)
    for c in range(NCB):
        for k in range(lpb):
            r = slice(k * LRU_BLOCK_DIM, (k + 1) * LRU_BLOCK_DIM)
            wbd = wbd.at[c, r, r].set(rg_wx[0, c * lpb + k])
            wbd = wbd.at[c, r, CB + k * LRU_BLOCK_DIM:CB + (k + 1) * LRU_BLOCK_DIM].set(
                rg_wa[0, c * lpb + k])
    wbd = wbd.astype(_bf16)
    pa = _col_blocks(proj_a[0].astype(_bf16))
    pb = _col_blocks(proj_b[0].astype(_bf16))
    wo = _col_blocks(w_out[0].astype(_bf16))

    zeros = jnp.zeros((D_MODEL,), _f32)
    cvec = jnp.stack([conv_b[0], rg_bx[0].reshape(-1), rg_ba[0].reshape(-1), rg_lambda[0],
                      jnp.tile(hg_norm_g[0], D_MODEL // HEAD_DIM), zeros, zeros, zeros])
    cw = conv_w[0]
    lbl = hg_lb_logits
    bm = b_merge[0].reshape(2, D_MODEL)
    ng = norm_g[0].reshape(1, D_MODEL)
    fg = final_norm_g.reshape(1, D_MODEL)

    x_spec = pl.BlockSpec((NB, TT, D_MODEL), lambda i, j: (i, j, 0))
    operands = (wz, wm, wbd, pa, pb, wo, cw, cvec, lbl, bm, ng, fg)
    return pl.pallas_call(
        _block_kernel,
        grid=(bsz // NB, seq // TT),
        in_specs=[x_spec] + [_resident(op.shape) for op in operands],
        out_specs=x_spec,
        out_shape=jax.ShapeDtypeStruct(x.shape, x.dtype),
        scratch_shapes=[
            pltpu.VMEM((M, D_MODEL), _bf16),
            pltpu.VMEM((M, NZ * CB), _f32),
            pltpu.VMEM((M, NZ * CB), _f32),
            pltpu.VMEM((NB, PITCH, CB), _f32),
            pltpu.VMEM((NCB, NB, 8, CB), _f32),
            pltpu.VMEM((LT, NB * PITCH, LANES), _f32),
            pltpu.VMEM((LT, NB * PITCH, LANES), _f32),
            pltpu.VMEM((LT, NB * PITCH, LANES), _f32),
            pltpu.VMEM((LT, NB * PITCH, LANES), _f32),
            pltpu.VMEM((NCB, NB, CB), _f32),
            pltpu.VMEM((NCB, NB, HPB, HEAD_DIM, HEAD_DIM), _f32),
            pltpu.VMEM((M, D_MODEL), _bf16),
            pltpu.VMEM((M, D_MODEL), _bf16),
            pltpu.VMEM((M, D_MODEL), _bf16),
        ],
        compiler_params=pltpu.CompilerParams(
            dimension_semantics=("arbitrary", "arbitrary"),
            vmem_limit_bytes=VMEM_LIMIT_BYTES),
        name="hawk_hgrn2_block",
    )(x, *operands)
```

```python
import jax
import jax.numpy as jnp
from jax import lax
from jax.experimental import pallas as pl
from jax.experimental.pallas import tpu as pltpu

D_MODEL = 1024
LRU_BLOCK_DIM = 128
CONV_WIDTH = 4
LRU_C = 8.0
HEAD_DIM = 128
CHUNK = 64
HG_SCALE = HEAD_DIM ** -0.5
EPS = 1e-6

NB = 8
TT = CHUNK
M = NB * TT
MH = M // 2
CB = 256
NCB = D_MODEL // CB
HPB = CB // HEAD_DIM
LANES = 128
LT = CB // LANES
PITCH = TT + 8
SCAN_CHUNK = 8
VMEM_LIMIT_BYTES = 58 * 1024 * 1024

Z_XA, Z_GA, Z_Q, Z_F, Z_I, Z_GB = range(6)
NZ = 6

_f32 = jnp.float32
_bf16 = jnp.bfloat16


def _dot(a, b):
    return jnp.dot(a, b, preferred_element_type=_f32)


def _dot_nt(a, b):
    return lax.dot_general(a, b, (((1,), (1,)), ((), ())), preferred_element_type=_f32)


def _dot_tn(a, b):
    return lax.dot_general(a, b, (((0,), (0,)), ((), ())), preferred_element_type=_f32)


def _sigmoid(x):
    return 1.0 / (1.0 + jnp.exp(-x))


def _silu(x):
    return x * _sigmoid(x)


def _sqrt_nonneg(x):
    return jnp.where(x > 0.0, x * lax.rsqrt(x), 0.0)


def _rows(b):
    return slice(b * TT, (b + 1) * TT)


def _to_pitched(scr, b, val):
    for l in range(LT):
        scr[l, b * PITCH:b * PITCH + TT, :] = val[:, l * LANES:(l + 1) * LANES]


def _from_pitched(scr, b):
    return jnp.concatenate([scr[l, b * PITCH:b * PITCH + TT, :] for l in range(LT)], axis=1)


def _emit_interleaved(primary, secondary):
    done = 0
    for i, piece in enumerate(primary):
        piece()
        target = (i + 1) * len(secondary) // len(primary)
        while done < target:
            secondary[done]()
            done += 1
    for piece in secondary[done:]:
        piece()


def _block_kernel(x_ref, wz_ref, wm_ref, wbd_ref, pa_ref, pb_ref, wo_ref,
                  cw_ref, cvec_ref, lbl_ref, bm_ref, ng_ref, fg_ref,
                  o_ref,
                  hb_scr, z0_scr, z1_scr, gm_scr, gt_scr, xc_scr, tail_scr,
                  a_scr, u_scr, h_scr, lf_scr, bc_scr,
                  hcar_scr, st_scr, ya_scr, yb_scr, mix_scr):
    j = pl.program_id(1)

    @pl.when(j == 0)
    def _reset_carries():
        tail_scr[...] = jnp.zeros_like(tail_scr)
        hcar_scr[...] = jnp.zeros_like(hcar_scr)
        st_scr[...] = jnp.zeros_like(st_scr)

    ng = ng_ref[...]
    for b in range(NB):
        xb = x_ref[b]
        ms = jnp.mean(xb * xb, axis=-1, keepdims=True)
        hb_scr[_rows(b), :] = ((xb * lax.rsqrt(ms + EPS)) * ng).astype(_bf16)

    causal = (lax.broadcasted_iota(jnp.int32, (TT, TT), 0)
              >= lax.broadcasted_iota(jnp.int32, (TT, TT), 1))
    z_bufs = (z0_scr, z1_scr)

    def project_pieces(c):
        z_scr = z_bufs[c % 2]

        def piece(g, r):
            def run():
                rs = slice(r * MH, (r + 1) * MH)
                gs = slice(g * CB, (g + 1) * CB)
                z_scr[rs, gs] = _dot(hb_scr[rs, :], wz_ref[c, :, gs])
            return run
        return [piece(g, r) for g in range(NZ) for r in range(M // MH)]

    def merge_gate_pieces(n):
        def piece(k, r):
            def run():
                rs = slice(r * MH, (r + 1) * MH)
                ks = slice(k * CB, (k + 1) * CB)
                gm_scr[rs, (2 * n + k) * CB:(2 * n + k + 1) * CB] = _dot(hb_scr[rs, :], wm_ref[n, :, ks])
            return run
        return [piece(k, r) for k in range(2) for r in range(M // MH)]

    def channel_block_pieces(c):
        z_scr = z_bufs[c % 2]
        cs = slice(c * CB, (c + 1) * CB)

        def zcols(g, b):
            return z_scr[_rows(b), g * CB:(g + 1) * CB]

        conv_b, bx, ba, lam, hg_g = (cvec_ref[k:k + 1, cs] for k in range(5))
        cw = cw_ref[:, cs]
        xa = [None] * NB
        kk = [None] * NB

        def conv(b):
            def run():
                xc_scr[b, 0:8, :] = tail_scr[c, b]
                xc_scr[b, 8:8 + TT, :] = zcols(Z_XA, b)
                acc = conv_b + cw[3:4] * xc_scr[b, 8:8 + TT, :]
                for k in range(CONV_WIDTH - 1):
                    acc = acc + cw[k:k + 1] * xc_scr[b, 5 + k:5 + k + TT, :]
                xa[b] = acc
                tail_scr[c, b] = xc_scr[b, TT:TT + 8, :]
            return run

        def gate_dot():
            gt_scr[...] = _dot(jnp.concatenate(xa, axis=0).astype(_bf16), wbd_ref[c])

        def gates(b):
            def run():
                neg_lam = -lam
                softplus = jnp.maximum(neg_lam, 0.0) + jnp.log(1.0 + jnp.exp(-jnp.abs(neg_lam)))
                gate_i = _sigmoid(gt_scr[_rows(b), :CB] + bx)
                gate_r = _sigmoid(gt_scr[_rows(b), CB:] + ba)
                a = jnp.exp(((-LRU_C) * softplus) * gate_r)
                mult = _sqrt_nonneg(1.0 - a * a)
                _to_pitched(a_scr, b, a)
                _to_pitched(u_scr, b, mult * gate_i * xa[b])
            return run

        def forget(b):
            def run():
                lbl = lbl_ref[:, cs]
                lmax = jnp.maximum(lbl[0:1], lbl[1:2])
                e0 = jnp.exp(lbl[0:1] - lmax)
                e1 = jnp.exp(lbl[1:2] - lmax)
                lb = e0 / (e0 + e1)
                f = lb + (1.0 - lb) * _sigmoid(zcols(Z_F, b))
                _to_pitched(lf_scr, b, jnp.log(f))
                kk[b] = 1.0 - f
            return run

        cums = [jnp.zeros((NB, LANES), _f32) for _ in range(LT)]
        hs = [None] * LT

        def cumsum_steps(t0):
            def run():
                for t in range(t0, t0 + SCAN_CHUNK):
                    rows = pl.ds(t, NB, stride=PITCH)
                    for l in range(LT):
                        cums[l] = cums[l] + lf_scr[l, rows, :]
                        bc_scr[l, rows, :] = cums[l]
            return run

        def lru_steps(t0):
            def run():
                if t0 == 0:
                    h0 = hcar_scr[c]
                    for l in range(LT):
                        hs[l] = h0[:, l * LANES:(l + 1) * LANES]
                for t in range(t0, t0 + SCAN_CHUNK):
                    rows = pl.ds(t, NB, stride=PITCH)
                    for l in range(LT):
                        hs[l] = a_scr[l, rows, :] * hs[l] + u_scr[l, rows, :]
                        h_scr[l, rows, :] = hs[l]
                if t0 + SCAN_CHUNK == TT:
                    hcar_scr[c] = jnp.concatenate(hs, axis=1)
            return run

        def gated_a(b):
            def run():
                ya_scr[_rows(b), cs] = (_from_pitched(h_scr, b) * _silu(zcols(Z_GA, b))).astype(_bf16)
            return run

        def attention(b):
            def run():
                bc = _from_pitched(bc_scr, b)
                b_mid = bc[CHUNK // 2:CHUNK // 2 + 1]
                b_last = bc[CHUNK - 1:CHUNK]
                q_in = _silu(zcols(Z_Q, b)) * jnp.exp(bc - b_mid) * HG_SCALE
                k_in = kk[b] * jnp.exp(b_mid - bc)
                q_inter = (q_in * jnp.exp(b_mid)).astype(_bf16)
                k_state = (k_in * jnp.exp(b_last - b_mid)).astype(_bf16)
                decay = jnp.exp(b_last)
                q_in_b = q_in.astype(_bf16)
                k_in_b = k_in.astype(_bf16)
                vb = zcols(Z_I, b).astype(_bf16)
                o_heads = []
                for hh in range(HPB):
                    ls = slice(hh * HEAD_DIM, (hh + 1) * HEAD_DIM)
                    att = jnp.where(causal, _dot_nt(q_in_b[:, ls], k_in_b[:, ls]), 0.0).astype(_bf16)
                    st = st_scr[c, b, hh]
                    o_h = _dot(att, vb[:, ls]) + _dot_nt(q_inter[:, ls], st.astype(_bf16))
                    st_scr[c, b, hh] = st * decay[:, ls] + _dot_tn(vb[:, ls], k_state[:, ls])
                    ms = jnp.mean(o_h * o_h, axis=-1, keepdims=True)
                    o_heads.append(o_h * lax.rsqrt(ms + EPS))
                o_n = jnp.concatenate(o_heads, axis=1) * hg_g
                yb_scr[_rows(b), cs] = (o_n * _silu(zcols(Z_GB, b))).astype(_bf16)
            return run

        steps = range(0, TT, SCAN_CHUNK)
        pieces = [conv(b) for b in range(NB)] + [gate_dot] + [forget(b) for b in range(NB)]
        for b, t0 in zip(range(NB), steps):
            pieces += [gates(b), cumsum_steps(t0)]
        for b, t0 in zip(range(NB), steps):
            pieces += [attention(b), lru_steps(t0)]
        pieces += [gated_a(b) for b in range(NB)]
        return pieces

    for piece in project_pieces(0):
        piece()
    for c in range(NCB):
        filler = project_pieces(c + 1) if c + 1 < NCB else []
        if c >= NCB - 2:
            base = 2 * (c - (NCB - 2))
            filler = filler + merge_gate_pieces(base) + merge_gate_pieces(base + 1)
        _emit_interleaved(channel_block_pieces(c), filler)

    for n in range(NCB):
        cs = slice(n * CB, (n + 1) * CB)
        out_a = _dot(ya_scr[...], pa_ref[n])
        out_b = _dot(yb_scr[...], pb_ref[n])
        g_a = _sigmoid(gm_scr[:, 2 * n * CB:(2 * n + 1) * CB] + bm_ref[0:1, cs])
        g_b = _sigmoid(gm_scr[:, (2 * n + 1) * CB:(2 * n + 2) * CB] + bm_ref[1:2, cs])
        mix_scr[:, cs] = (g_a * out_a + g_b * out_b).astype(_bf16)

    ssq = [jnp.zeros((TT, 1), _f32) for _ in range(NB)]
    for n in range(NCB):
        cs = slice(n * CB, (n + 1) * CB)
        proj = _dot(mix_scr[...], wo_ref[n])
        for b in range(NB):
            res = x_ref[b, :, cs] + proj[_rows(b)]
            o_ref[b, :, cs] = res
            ssq[b] = ssq[b] + jnp.sum(res * res, axis=-1, keepdims=True)
    fg = fg_ref[...]
    for b in range(NB):
        o_ref[b] = o_ref[b] * lax.rsqrt(ssq[b] * (1.0 / D_MODEL) + EPS) * fg


def _col_blocks(w):
    k, n = w.shape
    return w.reshape(k, n // CB, CB).transpose(1, 0, 2)


def _resident(shape):
    zeros = (0,) * len(shape)
    return pl.BlockSpec(shape, lambda i, j: zeros, pipeline_mode=pl.Buffered(1))


@jax.jit
def kernel(x, w_in, b_merge, conv_w, conv_b, rg_wx, rg_bx, rg_wa, rg_ba, rg_lambda,
           hg_lb_logits, hg_norm_g, proj_a, proj_b, w_out, norm_g, final_norm_g):
    bsz, seq, d = x.shape
    assert d == D_MODEL and bsz % NB == 0 and seq % TT == 0
    assert w_in.shape[0] == 1, "single-layer block"

    w_in_b = w_in[0].astype(_bf16).reshape(D_MODEL, 8, NCB, CB)
    wz = w_in_b[:, :NZ].transpose(2, 0, 1, 3).reshape(NCB, D_MODEL, NZ * CB)
    wm = w_in_b[:, NZ:].transpose(2, 0, 1, 3).reshape(NCB, D_MODEL, 2 * CB)
    lpb = CB // LRU_BLOCK_DIM
    wbd = jnp.zeros((NCB, CB, 2 * CB), _f32)
    for c in range(NCB):
        for k in range(lpb):
            r = slice(k * LRU_BLOCK_DIM, (k + 1) * LRU_BLOCK_DIM)
            wbd = wbd.at[c, r, r].set(rg_wx[0, c * lpb + k])
            wbd = wbd.at[c, r, CB + k * LRU_BLOCK_DIM:CB + (k + 1) * LRU_BLOCK_DIM].set(
                rg_wa[0, c * lpb + k])
    wbd = wbd.astype(_bf16)
    pa = _col_blocks(proj_a[0].astype(_bf16))
    pb = _col_blocks(proj_b[0].astype(_bf16))
    wo = _col_blocks(w_out[0].astype(_bf16))

    zeros = jnp.zeros((D_MODEL,), _f32)
    cvec = jnp.stack([conv_b[0], rg_bx[0].reshape(-1), rg_ba[0].reshape(-1), rg_lambda[0],
                      jnp.tile(hg_norm_g[0], D_MODEL // HEAD_DIM), zeros, zeros, zeros])
    cw = conv_w[0]
    lbl = hg_lb_logits
    bm = b_merge[0].reshape(2, D_MODEL)
    ng = norm_g[0].reshape(1, D_MODEL)
    fg = final_norm_g.reshape(1, D_MODEL)

    x_spec = pl.BlockSpec((NB, TT, D_MODEL), lambda i, j: (i, j, 0))
    operands = (wz, wm, wbd, pa, pb, wo, cw, cvec, lbl, bm, ng, fg)
    pitched = pltpu.VMEM((LT, NB * PITCH, LANES), _f32)
    return pl.pallas_call(
        _block_kernel,
        grid=(bsz // NB, seq // TT),
        in_specs=[x_spec] + [_resident(op.shape) for op in operands],
        out_specs=x_spec,
        out_shape=jax.ShapeDtypeStruct(x.shape, x.dtype),
        scratch_shapes=[
            pltpu.VMEM((M, D_MODEL), _bf16),
            pltpu.VMEM((M, NZ * CB), _f32),
            pltpu.VMEM((M, NZ * CB), _f32),
            pltpu.VMEM((M, 2 * D_MODEL), _f32),
            pltpu.VMEM((M, 2 * CB), _f32),
            pltpu.VMEM((NB, PITCH, CB), _f32),
            pltpu.VMEM((NCB, NB, 8, CB), _f32),
            pitched, pitched, pitched, pitched, pitched,
            pltpu.VMEM((NCB, NB, CB), _f32),
            pltpu.VMEM((NCB, NB, HPB, HEAD_DIM, HEAD_DIM), _f32),
            pltpu.VMEM((M, D_MODEL), _bf16),
            pltpu.VMEM((M, D_MODEL), _bf16),
            pltpu.VMEM((M, D_MODEL), _bf16),
        ],
        compiler_params=pltpu.CompilerParams(
            dimension_semantics=("arbitrary", "arbitrary"),
            vmem_limit_bytes=VMEM_LIMIT_BYTES),
        name="hawk_hgrn2_block",
    )(x, *operands)
```

```python
import jax
import jax.numpy as jnp
from jax import lax
from jax.experimental import pallas as pl
from jax.experimental.pallas import tpu as pltpu

D_MODEL = 1024
LRU_BLOCK_DIM = 128
CONV_WIDTH = 4
LRU_C = 8.0
HEAD_DIM = 128
CHUNK = 64
HG_SCALE = HEAD_DIM ** -0.5
EPS = 1e-6

NB = 8
TT = CHUNK
M = NB * TT
MH = M // 2
CB = 256
NCB = D_MODEL // CB
HPB = CB // HEAD_DIM
LANES = 128
LT = CB // LANES
PITCH = TT + 8
SCAN_CHUNK = 8
VMEM_LIMIT_BYTES = 58 * 1024 * 1024

G_XA, G_GA, G_Q, G_F, G_I, G_GB, G_MA, G_MB = range(8)
Z_GROUPS = (G_XA, G_GA, G_Q, G_F, G_I, G_GB)
Z_XA, Z_GA, Z_Q, Z_F, Z_I, Z_GB = range(6)
NZ = len(Z_GROUPS)

_f32 = jnp.float32
_bf16 = jnp.bfloat16


def _dot(a, b):
    return jnp.dot(a, b, preferred_element_type=_f32)


def _dot_nt(a, b):
    return lax.dot_general(a, b, (((1,), (1,)), ((), ())), preferred_element_type=_f32)


def _dot_tn(a, b):
    return lax.dot_general(a, b, (((0,), (0,)), ((), ())), preferred_element_type=_f32)


def _sigmoid(x):
    return 0.5 * jnp.tanh(0.5 * x) + 0.5


def _silu(x):
    h = 0.5 * x
    return h * jnp.tanh(h) + h


def _sqrt_nonneg(x):
    return jnp.where(x > 0.0, x * lax.rsqrt(x), 0.0)


def _rows(b):
    return slice(b * TT, (b + 1) * TT)


def _cols(n):
    return slice(n * CB, (n + 1) * CB)


def _to_pitched(scr, b, val):
    for l in range(LT):
        scr[l, b * PITCH:b * PITCH + TT, :] = val[:, l * LANES:(l + 1) * LANES]


def _from_pitched(scr, b):
    return jnp.concatenate([scr[l, b * PITCH:b * PITCH + TT, :] for l in range(LT)], axis=1)


def _emit_interleaved(primary, secondary):
    done = 0
    for i, piece in enumerate(primary):
        piece()
        target = (i + 1) * len(secondary) // len(primary)
        while done < target:
            secondary[done]()
            done += 1
    for piece in secondary[done:]:
        piece()


def _block_kernel(x_ref, w_in_ref, wbd_ref, pa_ref, pb_ref, wo_ref,
                  cw_ref, cvec_ref, lbl_ref, bm_ref, ng_ref, fg_ref,
                  o_ref,
                  hb_scr, z0_scr, z1_scr, gm_scr, gt_scr, xc_scr, tail_scr,
                  a_scr, u_scr, h_scr, lf_scr, bc_scr,
                  hcar_scr, st_scr, ya_scr, yb_scr, mix_scr):
    j = pl.program_id(1)

    @pl.when(j == 0)
    def _reset_carries():
        tail_scr[...] = jnp.zeros_like(tail_scr)
        hcar_scr[...] = jnp.zeros_like(hcar_scr)
        st_scr[...] = jnp.zeros_like(st_scr)

    ng = ng_ref[...]
    for b in range(NB):
        xb = x_ref[b]
        ms = jnp.mean(xb * xb, axis=-1, keepdims=True)
        hb_scr[_rows(b), :] = ((xb * lax.rsqrt(ms + EPS)) * ng).astype(_bf16)

    causal = (lax.broadcasted_iota(jnp.int32, (TT, TT), 0)
              >= lax.broadcasted_iota(jnp.int32, (TT, TT), 1))
    z_bufs = (z0_scr, z1_scr)

    def w_in_cols(group, c):
        start = group * D_MODEL + c * CB
        return w_in_ref[:, start:start + CB]

    def project_pieces(c):
        z_scr = z_bufs[c % 2]

        def piece(slot, r):
            def run():
                rs = slice(r * MH, (r + 1) * MH)
                z_scr[rs, _cols(slot)] = _dot(hb_scr[rs, :], w_in_cols(Z_GROUPS[slot], c))
            return run
        return [piece(slot, r) for slot in range(NZ) for r in range(M // MH)]

    def merge_gate_pieces(n):
        def piece(k, r):
            def run():
                rs = slice(r * MH, (r + 1) * MH)
                gm_scr[rs, _cols(2 * n + k)] = _dot(hb_scr[rs, :], w_in_cols((G_MA, G_MB)[k], n))
            return run
        return [piece(k, r) for k in range(2) for r in range(M // MH)]

    def channel_block_pieces(c):
        z_scr = z_bufs[c % 2]
        cs = _cols(c)

        def zcols(slot, b):
            return z_scr[_rows(b), _cols(slot)]

        conv_b, bx, ba, lam, hg_g = (cvec_ref[k:k + 1, cs] for k in range(5))
        cw = cw_ref[:, cs]
        xa = [None] * NB
        kk = [None] * NB

        def conv(b):
            def run():
                xc_scr[b, 0:8, :] = tail_scr[c, b]
                xc_scr[b, 8:8 + TT, :] = zcols(Z_XA, b)
                acc = conv_b + cw[3:4] * xc_scr[b, 8:8 + TT, :]
                for k in range(CONV_WIDTH - 1):
                    acc = acc + cw[k:k + 1] * xc_scr[b, 5 + k:5 + k + TT, :]
                xa[b] = acc
                tail_scr[c, b] = xc_scr[b, TT:TT + 8, :]
            return run

        def gate_dot():
            gt_scr[...] = _dot(jnp.concatenate(xa, axis=0).astype(_bf16), wbd_ref[c])

        def gates(b):
            def run():
                neg_lam = -lam
                softplus = jnp.maximum(neg_lam, 0.0) + jnp.log(1.0 + jnp.exp(-jnp.abs(neg_lam)))
                gate_i = _sigmoid(gt_scr[_rows(b), :CB] + bx)
                gate_r = _sigmoid(gt_scr[_rows(b), CB:] + ba)
                a = jnp.exp(((-LRU_C) * softplus) * gate_r)
                mult = _sqrt_nonneg(1.0 - a * a)
                _to_pitched(a_scr, b, a)
                _to_pitched(u_scr, b, mult * gate_i * xa[b])
            return run

        def forget(b):
            def run():
                lbl = lbl_ref[:, cs]
                lmax = jnp.maximum(lbl[0:1], lbl[1:2])
                e0 = jnp.exp(lbl[0:1] - lmax)
                e1 = jnp.exp(lbl[1:2] - lmax)
                lb = e0 / (e0 + e1)
                f = lb + (1.0 - lb) * _sigmoid(zcols(Z_F, b))
                _to_pitched(lf_scr, b, jnp.log(f))
                kk[b] = 1.0 - f
            return run

        cums = [jnp.zeros((NB, LANES), _f32) for _ in range(LT)]
        hs = [None] * LT

        def cumsum_steps(t0):
            def run():
                for t in range(t0, t0 + SCAN_CHUNK):
                    rows = pl.ds(t, NB, stride=PITCH)
                    for l in range(LT):
                        cums[l] = cums[l] + lf_scr[l, rows, :]
                        bc_scr[l, rows, :] = cums[l]
            return run

        def lru_steps(t0):
            def run():
                if t0 == 0:
                    h0 = hcar_scr[c]
                    for l in range(LT):
                        hs[l] = h0[:, l * LANES:(l + 1) * LANES]
                for t in range(t0, t0 + SCAN_CHUNK):
                    rows = pl.ds(t, NB, stride=PITCH)
                    for l in range(LT):
                        hs[l] = a_scr[l, rows, :] * hs[l] + u_scr[l, rows, :]
                        h_scr[l, rows, :] = hs[l]
                if t0 + SCAN_CHUNK == TT:
                    hcar_scr[c] = jnp.concatenate(hs, axis=1)
            return run

        def gated_a(b):
            def run():
                ya_scr[_rows(b), cs] = (_from_pitched(h_scr, b) * _silu(zcols(Z_GA, b))).astype(_bf16)
            return run

        def attention(b):
            def run():
                bc = _from_pitched(bc_scr, b)
                b_mid = bc[CHUNK // 2:CHUNK // 2 + 1]
                b_last = bc[CHUNK - 1:CHUNK]
                q_in = _silu(zcols(Z_Q, b)) * jnp.exp(bc - b_mid) * HG_SCALE
                k_in = kk[b] * jnp.exp(b_mid - bc)
                q_inter = (q_in * jnp.exp(b_mid)).astype(_bf16)
                k_state = (k_in * jnp.exp(b_last - b_mid)).astype(_bf16)
                decay = jnp.exp(b_last)
                q_in_b = q_in.astype(_bf16)
                k_in_b = k_in.astype(_bf16)
                vb = zcols(Z_I, b).astype(_bf16)
                o_heads = []
                for hh in range(HPB):
                    ls = slice(hh * HEAD_DIM, (hh + 1) * HEAD_DIM)
                    att = jnp.where(causal, _dot_nt(q_in_b[:, ls], k_in_b[:, ls]), 0.0).astype(_bf16)
                    st = st_scr[c, b, hh]
                    o_h = _dot(att, vb[:, ls]) + _dot_nt(q_inter[:, ls], st.astype(_bf16))
                    st_scr[c, b, hh] = st * decay[:, ls] + _dot_tn(vb[:, ls], k_state[:, ls])
                    ms = jnp.mean(o_h * o_h, axis=-1, keepdims=True)
                    o_heads.append(o_h * lax.rsqrt(ms + EPS))
                o_n = jnp.concatenate(o_heads, axis=1) * hg_g
                yb_scr[_rows(b), cs] = (o_n * _silu(zcols(Z_GB, b))).astype(_bf16)
            return run

        steps = range(0, TT, SCAN_CHUNK)
        pieces = [conv(b) for b in range(NB)] + [gate_dot] + [forget(b) for b in range(NB)]
        for b, t0 in zip(range(NB), steps):
            pieces += [gates(b), cumsum_steps(t0)]
        for b, t0 in zip(range(NB), steps):
            pieces += [attention(b), lru_steps(t0)]
        pieces += [gated_a(b) for b in range(NB)]
        return pieces

    for piece in project_pieces(0):
        piece()
    for c in range(NCB):
        filler = project_pieces(c + 1) if c + 1 < NCB else []
        if c >= NCB - 2:
            base = 2 * (c - (NCB - 2))
            filler = filler + merge_gate_pieces(base) + merge_gate_pieces(base + 1)
        _emit_interleaved(channel_block_pieces(c), filler)

    for n in range(NCB):
        cs = _cols(n)
        out_a = _dot(ya_scr[...], pa_ref[:, cs])
        out_b = _dot(yb_scr[...], pb_ref[:, cs])
        g_a = _sigmoid(gm_scr[:, _cols(2 * n)] + bm_ref[0:1, cs])
        g_b = _sigmoid(gm_scr[:, _cols(2 * n + 1)] + bm_ref[1:2, cs])
        mix_scr[:, cs] = (g_a * out_a + g_b * out_b).astype(_bf16)

    ssq = [jnp.zeros((TT, 1), _f32) for _ in range(NB)]
    for n in range(NCB):
        cs = _cols(n)
        proj = _dot(mix_scr[...], wo_ref[:, cs])
        for b in range(NB):
            res = x_ref[b, :, cs] + proj[_rows(b)]
            o_ref[b, :, cs] = res
            ssq[b] = ssq[b] + jnp.sum(res * res, axis=-1, keepdims=True)
    fg = fg_ref[...]
    for b in range(NB):
        o_ref[b] = o_ref[b] * lax.rsqrt(ssq[b] * (1.0 / D_MODEL) + EPS) * fg


def _resident(shape):
    zeros = (0,) * len(shape)
    return pl.BlockSpec(shape, lambda i, j: zeros, pipeline_mode=pl.Buffered(1))


@jax.jit
def kernel(x, w_in, b_merge, conv_w, conv_b, rg_wx, rg_bx, rg_wa, rg_ba, rg_lambda,
           hg_lb_logits, hg_norm_g, proj_a, proj_b, w_out, norm_g, final_norm_g):
    bsz, seq, d = x.shape
    assert d == D_MODEL and bsz % NB == 0 and seq % TT == 0
    assert w_in.shape[0] == 1, "single-layer block"

    w_in_b = w_in[0].astype(_bf16)
    lpb = CB // LRU_BLOCK_DIM
    wbd = jnp.zeros((NCB, CB, 2 * CB), _bf16)
    for c in range(NCB):
        for k in range(lpb):
            r = slice(k * LRU_BLOCK_DIM, (k + 1) * LRU_BLOCK_DIM)
            wbd = wbd.at[c, r, r].set(rg_wx[0, c * lpb + k].astype(_bf16))
            wbd = wbd.at[c, r, CB + k * LRU_BLOCK_DIM:CB + (k + 1) * LRU_BLOCK_DIM].set(
                rg_wa[0, c * lpb + k].astype(_bf16))
    pa = proj_a[0].astype(_bf16)
    pb = proj_b[0].astype(_bf16)
    wo = w_out[0].astype(_bf16)

    zeros = jnp.zeros((D_MODEL,), _f32)
    cvec = jnp.stack([conv_b[0], rg_bx[0].reshape(-1), rg_ba[0].reshape(-1), rg_lambda[0],
                      jnp.tile(hg_norm_g[0], D_MODEL // HEAD_DIM), zeros, zeros, zeros])
    cw = conv_w[0]
    lbl = hg_lb_logits
    bm = b_merge[0].reshape(2, D_MODEL)
    ng = norm_g[0].reshape(1, D_MODEL)
    fg = final_norm_g.reshape(1, D_MODEL)

    x_spec = pl.BlockSpec((NB, TT, D_MODEL), lambda i, j: (i, j, 0))
    operands = (w_in_b, wbd, pa, pb, wo, cw, cvec, lbl, bm, ng, fg)
    pitched = pltpu.VMEM((LT, NB * PITCH, LANES), _f32)
    return pl.pallas_call(
        _block_kernel,
        grid=(bsz // NB, seq // TT),
        in_specs=[x_spec] + [_resident(op.shape) for op in operands],
        out_specs=x_spec,
        out_shape=jax.ShapeDtypeStruct(x.shape, x.dtype),
        scratch_shapes=[
            pltpu.VMEM((M, D_MODEL), _bf16),
            pltpu.VMEM((M, NZ * CB), _f32),
            pltpu.VMEM((M, NZ * CB), _f32),
            pltpu.VMEM((M, 2 * D_MODEL), _f32),
            pltpu.VMEM((M, 2 * CB), _f32),
            pltpu.VMEM((NB, PITCH, CB), _f32),
            pltpu.VMEM((NCB, NB, 8, CB), _f32),
            pitched, pitched, pitched, pitched, pitched,
            pltpu.VMEM((NCB, NB, CB), _f32),
            pltpu.VMEM((NCB, NB, HPB, HEAD_DIM, HEAD_DIM), _f32),
            pltpu.VMEM((M, D_MODEL), _bf16),
            pltpu.VMEM((M, D_MODEL), _bf16),
            pltpu.VMEM((M, D_MODEL), _bf16),
        ],
        compiler_params=pltpu.CompilerParams(
            dimension_semantics=("arbitrary", "arbitrary"),
            vmem_limit_bytes=VMEM_LIMIT_BYTES),
        name="hawk_hgrn2_block",
    )(x, *operands)
```

```python
import jax
import jax.numpy as jnp
from jax import lax
from jax.experimental import pallas as pl
from jax.experimental.pallas import tpu as pltpu

D_MODEL = 1024
LRU_BLOCK_DIM = 128
CONV_WIDTH = 4
LRU_C = 8.0
HEAD_DIM = 128
CHUNK = 64
HG_SCALE = HEAD_DIM ** -0.5
EPS = 1e-6

NB = 8
TT = CHUNK
M = NB * TT
MH = M // 2
CB = 256
NCB = D_MODEL // CB
HPB = CB // HEAD_DIM
LANES = 128
LT = CB // LANES
PITCH = TT + 8
SCAN_CHUNK = 8
VMEM_LIMIT_BYTES = 58 * 1024 * 1024

G_XA, G_GA, G_Q, G_F, G_I, G_GB, G_MA, G_MB = range(8)
Z_GROUPS = (G_XA, G_GA, G_Q, G_F, G_I, G_GB)
Z_XA, Z_GA, Z_Q, Z_F, Z_I, Z_GB = range(6)
NZ = len(Z_GROUPS)

_f32 = jnp.float32
_bf16 = jnp.bfloat16


def _dot(a, b):
    return jnp.dot(a, b, preferred_element_type=_f32)


def _dot_nt(a, b):
    return lax.dot_general(a, b, (((1,), (1,)), ((), ())), preferred_element_type=_f32)


def _dot_tn(a, b):
    return lax.dot_general(a, b, (((0,), (0,)), ((), ())), preferred_element_type=_f32)


def _sigmoid(x):
    return 0.5 * jnp.tanh(0.5 * x) + 0.5


def _silu(x):
    h = 0.5 * x
    return h * jnp.tanh(h) + h


def _sqrt_nonneg(x):
    return jnp.where(x > 0.0, x * lax.rsqrt(x), 0.0)


def _rows(b):
    return slice(b * TT, (b + 1) * TT)


def _cols(n):
    return slice(n * CB, (n + 1) * CB)


def _to_pitched(scr, b, val):
    for l in range(LT):
        scr[l, b * PITCH:b * PITCH + TT, :] = val[:, l * LANES:(l + 1) * LANES]


def _from_pitched(scr, b):
    return jnp.concatenate([scr[l, b * PITCH:b * PITCH + TT, :] for l in range(LT)], axis=1)


def _emit_interleaved(primary, secondary):
    done = 0
    for i, piece in enumerate(primary):
        piece()
        target = (i + 1) * len(secondary) // len(primary)
        while done < target:
            secondary[done]()
            done += 1
    for piece in secondary[done:]:
        piece()


def _block_kernel(x_ref, w_in_ref, wbd_ref, pa_ref, pb_ref, wo_ref,
                  cw_ref, cvec_ref, lbl_ref, bm_ref, ng_ref, fg_ref,
                  o_ref,
                  hb_scr, z0_scr, z1_scr, gm_scr, gt_scr, xc_scr, tail_scr,
                  a_scr, u_scr, h_scr, lf_scr, bc_scr,
                  hcar_scr, st_scr, ya_scr, yb_scr, mix_scr):
    j = pl.program_id(1)

    @pl.when(j == 0)
    def _reset_carries():
        tail_scr[...] = jnp.zeros_like(tail_scr)
        hcar_scr[...] = jnp.zeros_like(hcar_scr)
        st_scr[...] = jnp.zeros_like(st_scr)

    ng = ng_ref[...]
    for b in range(NB):
        xb = x_ref[b]
        ms = jnp.mean(xb * xb, axis=-1, keepdims=True)
        hb_scr[_rows(b), :] = ((xb * lax.rsqrt(ms + EPS)) * ng).astype(_bf16)

    causal = (lax.broadcasted_iota(jnp.int32, (TT, TT), 0)
              >= lax.broadcasted_iota(jnp.int32, (TT, TT), 1))
    z_bufs = (z0_scr, z1_scr)

    def w_in_cols(group, c):
        start = group * D_MODEL + c * CB
        return w_in_ref[:, start:start + CB]

    def project_pieces(c):
        z_scr = z_bufs[c % 2]

        def piece(slot, r):
            def run():
                rs = slice(r * MH, (r + 1) * MH)
                z_scr[rs, _cols(slot)] = _dot(hb_scr[rs, :], w_in_cols(Z_GROUPS[slot], c))
            return run
        return [piece(slot, r) for slot in range(NZ) for r in range(M // MH)]

    def merge_gate_pieces(n):
        def piece(k, r):
            def run():
                rs = slice(r * MH, (r + 1) * MH)
                gm_scr[rs, _cols(2 * n + k)] = _dot(hb_scr[rs, :], w_in_cols((G_MA, G_MB)[k], n))
            return run
        return [piece(k, r) for k in range(2) for r in range(M // MH)]

    def channel_block_pieces(c):
        z_scr = z_bufs[c % 2]
        cs = _cols(c)

        def zcols(slot, b):
            return z_scr[_rows(b), _cols(slot)]

        conv_b, bx, ba, lam, hg_g = (cvec_ref[k:k + 1, cs] for k in range(5))
        cw = cw_ref[:, cs]
        xa = [None] * NB
        kk = [None] * NB

        def conv(b):
            def run():
                xc_scr[b, 0:8, :] = tail_scr[c, b]
                xc_scr[b, 8:8 + TT, :] = zcols(Z_XA, b)
                acc = conv_b + cw[3:4] * xc_scr[b, 8:8 + TT, :]
                for k in range(CONV_WIDTH - 1):
                    acc = acc + cw[k:k + 1] * xc_scr[b, 5 + k:5 + k + TT, :]
                xa[b] = acc
                tail_scr[c, b] = xc_scr[b, TT:TT + 8, :]
            return run

        def gate_dot():
            gt_scr[...] = _dot(jnp.concatenate(xa, axis=0).astype(_bf16), wbd_ref[c])

        def gates(b):
            def run():
                neg_lam = -lam
                softplus = jnp.maximum(neg_lam, 0.0) + jnp.log(1.0 + jnp.exp(-jnp.abs(neg_lam)))
                gate_i = _sigmoid(gt_scr[_rows(b), :CB] + bx)
                gate_r = _sigmoid(gt_scr[_rows(b), CB:] + ba)
                a = jnp.exp(((-LRU_C) * softplus) * gate_r)
                mult = _sqrt_nonneg(1.0 - a * a)
                _to_pitched(a_scr, b, a)
                _to_pitched(u_scr, b, mult * gate_i * xa[b])
            return run

        def forget(b):
            def run():
                lbl = lbl_ref[:, cs]
                lmax = jnp.maximum(lbl[0:1], lbl[1:2])
                e0 = jnp.exp(lbl[0:1] - lmax)
                e1 = jnp.exp(lbl[1:2] - lmax)
                lb = e0 / (e0 + e1)
                f = lb + (1.0 - lb) * _sigmoid(zcols(Z_F, b))
                _to_pitched(lf_scr, b, jnp.log(f))
                kk[b] = 1.0 - f
            return run

        cums = [jnp.zeros((NB, LANES), _f32) for _ in range(LT)]
        hs = [None] * LT

        def cumsum_steps(t0):
            def run():
                for t in range(t0, t0 + SCAN_CHUNK):
                    rows = pl.ds(t, NB, stride=PITCH)
                    for l in range(LT):
                        cums[l] = cums[l] + lf_scr[l, rows, :]
                        bc_scr[l, rows, :] = cums[l]
            return run

        def lru_steps(t0):
            def run():
                if t0 == 0:
                    h0 = hcar_scr[c]
                    for l in range(LT):
                        hs[l] = h0[:, l * LANES:(l + 1) * LANES]
                for t in range(t0, t0 + SCAN_CHUNK):
                    rows = pl.ds(t, NB, stride=PITCH)
                    for l in range(LT):
                        hs[l] = a_scr[l, rows, :] * hs[l] + u_scr[l, rows, :]
                        h_scr[l, rows, :] = hs[l]
                if t0 + SCAN_CHUNK == TT:
                    hcar_scr[c] = jnp.concatenate(hs, axis=1)
            return run

        def gated_a(b):
            def run():
                ya_scr[_rows(b), cs] = (_from_pitched(h_scr, b) * _silu(zcols(Z_GA, b))).astype(_bf16)
            return run

        att_in = [None] * NB
        att_w = {}
        o_raw = {}
        heads = [(b, hh) for b in range(NB) for hh in range(HPB)]

        def head_cols(hh):
            return slice(hh * HEAD_DIM, (hh + 1) * HEAD_DIM)

        def att_prologue(b):
            def run():
                bc = _from_pitched(bc_scr, b)
                b_mid = bc[CHUNK // 2:CHUNK // 2 + 1]
                b_last = bc[CHUNK - 1:CHUNK]
                q_in = _silu(zcols(Z_Q, b)) * jnp.exp(bc - b_mid) * HG_SCALE
                k_in = kk[b] * jnp.exp(b_mid - bc)
                q_inter = (q_in * jnp.exp(b_mid)).astype(_bf16)
                k_state = (k_in * jnp.exp(b_last - b_mid)).astype(_bf16)
                att_in[b] = (q_in.astype(_bf16), k_in.astype(_bf16), q_inter, k_state,
                             zcols(Z_I, b).astype(_bf16), jnp.exp(b_last))
            return run

        def att_scores(b, hh):
            def run():
                q_in, k_in = att_in[b][0], att_in[b][1]
                ls = head_cols(hh)
                att_w[b, hh] = jnp.where(causal, _dot_nt(q_in[:, ls], k_in[:, ls]), 0.0).astype(_bf16)
            return run

        def att_output(b, hh):
            def run():
                q_inter, v = att_in[b][2], att_in[b][4]
                ls = head_cols(hh)
                st = st_scr[c, b, hh]
                o_raw[b, hh] = _dot(att_w[b, hh], v[:, ls]) + _dot_nt(q_inter[:, ls], st.astype(_bf16))
            return run

        def state_update(b, hh):
            def run():
                k_state, v, decay = att_in[b][3], att_in[b][4], att_in[b][5]
                ls = head_cols(hh)
                st_scr[c, b, hh] = st_scr[c, b, hh] * decay[:, ls] + _dot_tn(v[:, ls], k_state[:, ls])
            return run

        def gated_b(b):
            def run():
                o_heads = []
                for hh in range(HPB):
                    o_h = o_raw[b, hh]
                    ms = jnp.mean(o_h * o_h, axis=-1, keepdims=True)
                    o_heads.append(o_h * lax.rsqrt(ms + EPS))
                o_n = jnp.concatenate(o_heads, axis=1) * hg_g
                yb_scr[_rows(b), cs] = (o_n * _silu(zcols(Z_GB, b))).astype(_bf16)
            return run

        steps = range(0, TT, SCAN_CHUNK)
        pieces = [conv(b) for b in range(NB)] + [gate_dot] + [forget(b) for b in range(NB)]
        for b, t0 in zip(range(NB), steps):
            pieces += [gates(b), cumsum_steps(t0)]
        for b, t0 in zip(range(NB), steps):
            pieces += [att_prologue(b), lru_steps(t0)]
        pieces += [att_scores(b, hh) for b, hh in heads]
        pieces += [gated_a(b) for b in range(NB // 2)]
        pieces += [att_output(b, hh) for b, hh in heads]
        pieces += [gated_a(b) for b in range(NB // 2, NB)]
        pieces += [state_update(b, hh) for b, hh in heads]
        pieces += [gated_b(b) for b in range(NB)]
        return pieces

    for piece in project_pieces(0):
        piece()
    for c in range(NCB):
        filler = project_pieces(c + 1) if c + 1 < NCB else []
        if c >= NCB - 2:
            base = 2 * (c - (NCB - 2))
            filler = filler + merge_gate_pieces(base) + merge_gate_pieces(base + 1)
        _emit_interleaved(channel_block_pieces(c), filler)

    for n in range(NCB):
        cs = _cols(n)
        out_a = _dot(ya_scr[...], pa_ref[:, cs])
        out_b = _dot(yb_scr[...], pb_ref[:, cs])
        g_a = _sigmoid(gm_scr[:, _cols(2 * n)] + bm_ref[0:1, cs])
        g_b = _sigmoid(gm_scr[:, _cols(2 * n + 1)] + bm_ref[1:2, cs])
        mix_scr[:, cs] = (g_a * out_a + g_b * out_b).astype(_bf16)

    ssq = [jnp.zeros((TT, 1), _f32) for _ in range(NB)]
    for n in range(NCB):
        cs = _cols(n)
        proj = _dot(mix_scr[...], wo_ref[:, cs])
        for b in range(NB):
            res = x_ref[b, :, cs] + proj[_rows(b)]
            o_ref[b, :, cs] = res
            ssq[b] = ssq[b] + jnp.sum(res * res, axis=-1, keepdims=True)
    fg = fg_ref[...]
    for b in range(NB):
        o_ref[b] = o_ref[b] * lax.rsqrt(ssq[b] * (1.0 / D_MODEL) + EPS) * fg


def _resident(shape):
    zeros = (0,) * len(shape)
    return pl.BlockSpec(shape, lambda i, j: zeros, pipeline_mode=pl.Buffered(1))


@jax.jit
def kernel(x, w_in, b_merge, conv_w, conv_b, rg_wx, rg_bx, rg_wa, rg_ba, rg_lambda,
           hg_lb_logits, hg_norm_g, proj_a, proj_b, w_out, norm_g, final_norm_g):
    bsz, seq, d = x.shape
    assert d == D_MODEL and bsz % NB == 0 and seq % TT == 0
    assert w_in.shape[0] == 1, "single-layer block"

    w_in_b = w_in[0].astype(_bf16)
    lpb = CB // LRU_BLOCK_DIM
    wbd = jnp.zeros((NCB, CB, 2 * CB), _bf16)
    for c in range(NCB):
        for k in range(lpb):
            r = slice(k * LRU_BLOCK_DIM, (k + 1) * LRU_BLOCK_DIM)
            wbd = wbd.at[c, r, r].set(rg_wx[0, c * lpb + k].astype(_bf16))
            wbd = wbd.at[c, r, CB + k * LRU_BLOCK_DIM:CB + (k + 1) * LRU_BLOCK_DIM].set(
                rg_wa[0, c * lpb + k].astype(_bf16))
    pa = proj_a[0].astype(_bf16)
    pb = proj_b[0].astype(_bf16)
    wo = w_out[0].astype(_bf16)

    zeros = jnp.zeros((D_MODEL,), _f32)
    cvec = jnp.stack([conv_b[0], rg_bx[0].reshape(-1), rg_ba[0].reshape(-1), rg_lambda[0],
                      jnp.tile(hg_norm_g[0], D_MODEL // HEAD_DIM), zeros, zeros, zeros])
    cw = conv_w[0]
    lbl = hg_lb_logits
    bm = b_merge[0].reshape(2, D_MODEL)
    ng = norm_g[0].reshape(1, D_MODEL)
    fg = final_norm_g.reshape(1, D_MODEL)

    x_spec = pl.BlockSpec((NB, TT, D_MODEL), lambda i, j: (i, j, 0))
    operands = (w_in_b, wbd, pa, pb, wo, cw, cvec, lbl, bm, ng, fg)
    pitched = pltpu.VMEM((LT, NB * PITCH, LANES), _f32)
    return pl.pallas_call(
        _block_kernel,
        grid=(bsz // NB, seq // TT),
        in_specs=[x_spec] + [_resident(op.shape) for op in operands],
        out_specs=x_spec,
        out_shape=jax.ShapeDtypeStruct(x.shape, x.dtype),
        scratch_shapes=[
            pltpu.VMEM((M, D_MODEL), _bf16),
            pltpu.VMEM((M, NZ * CB), _f32),
            pltpu.VMEM((M, NZ * CB), _f32),
            pltpu.VMEM((M, 2 * D_MODEL), _f32),
            pltpu.VMEM((M, 2 * CB), _f32),
            pltpu.VMEM((NB, PITCH, CB), _f32),
            pltpu.VMEM((NCB, NB, 8, CB), _f32),
            pitched, pitched, pitched, pitched, pitched,
            pltpu.VMEM((NCB, NB, CB), _f32),
            pltpu.VMEM((NCB, NB, HPB, HEAD_DIM, HEAD_DIM), _f32),
            pltpu.VMEM((M, D_MODEL), _bf16),
            pltpu.VMEM((M, D_MODEL), _bf16),
            pltpu.VMEM((M, D_MODEL), _bf16),
        ],
        compiler_params=pltpu.CompilerParams(
            dimension_semantics=("arbitrary", "arbitrary"),
            vmem_limit_bytes=VMEM_LIMIT_BYTES),
        name="hawk_hgrn2_block",
    )(x, *operands)
```

```python
import jax
import jax.numpy as jnp
from jax import lax
from jax.experimental import pallas as pl
from jax.experimental.pallas import tpu as pltpu

D_MODEL = 1024
LRU_BLOCK_DIM = 128
CONV_WIDTH = 4
LRU_C = 8.0
HEAD_DIM = 128
CHUNK = 64
HG_SCALE = HEAD_DIM ** -0.5
EPS = 1e-6

NB = 8
TT = CHUNK
M = NB * TT
MH = M // 2
KH = D_MODEL // 2
CB = 256
NCB = D_MODEL // CB
HPB = CB // HEAD_DIM
LANES = 128
LT = CB // LANES
PITCH = TT + 8
SCAN_CHUNK = 8
VMEM_LIMIT_BYTES = 58 * 1024 * 1024

G_XA, G_GA, G_Q, G_F, G_I, G_GB, G_MA, G_MB = range(8)
Z_GROUPS = (G_XA, G_GA, G_Q, G_F, G_I, G_GB)
Z_XA, Z_GA, Z_Q, Z_F, Z_I, Z_GB = range(6)
NZ = len(Z_GROUPS)

_f32 = jnp.float32
_bf16 = jnp.bfloat16


def _dot(a, b):
    return jnp.dot(a, b, preferred_element_type=_f32)


def _dot_nt(a, b):
    return lax.dot_general(a, b, (((1,), (1,)), ((), ())), preferred_element_type=_f32)


def _dot_tn(a, b):
    return lax.dot_general(a, b, (((0,), (0,)), ((), ())), preferred_element_type=_f32)


def _sigmoid(x):
    return 0.5 * jnp.tanh(0.5 * x) + 0.5


def _silu(x):
    h = 0.5 * x
    return h * jnp.tanh(h) + h


def _sqrt_nonneg(x):
    return jnp.where(x > 0.0, x * lax.rsqrt(x), 0.0)


def _rows(b):
    return slice(b * TT, (b + 1) * TT)


def _cols(n):
    return slice(n * CB, (n + 1) * CB)


def _to_pitched(scr, b, val):
    for l in range(LT):
        scr[l, b * PITCH:b * PITCH + TT, :] = val[:, l * LANES:(l + 1) * LANES]


def _from_pitched(scr, b):
    return jnp.concatenate([scr[l, b * PITCH:b * PITCH + TT, :] for l in range(LT)], axis=1)


def _emit_interleaved(primary, secondary):
    total = sum(w for w, _ in primary)
    done, spent = 0, 0
    for w, piece in primary:
        piece()
        spent += w
        target = spent * len(secondary) // total
        while done < target:
            secondary[done]()
            done += 1
    for piece in secondary[done:]:
        piece()


def _block_kernel(x_ref, w_in_ref, wbd_ref, pa_ref, pb_ref, wo_ref,
                  cw_ref, cvec_ref, lbl_ref, bm_ref, ng_ref, fg_ref,
                  o_ref,
                  hb_scr, z0_scr, z1_scr, gm_scr, gt_scr, xc_scr, tail_scr,
                  a_scr, u_scr, h_scr, lf_scr, bc_scr,
                  hcar_scr, st_scr, ya_scr, yb_scr, mix_scr, acc_a_scr, acc_b_scr):
    j = pl.program_id(1)

    @pl.when(j == 0)
    def _reset_carries():
        tail_scr[...] = jnp.zeros_like(tail_scr)
        hcar_scr[...] = jnp.zeros_like(hcar_scr)
        st_scr[...] = jnp.zeros_like(st_scr)

    ng = ng_ref[...]
    rinv_in = [None] * NB

    def in_scale(b):
        def run():
            xb = x_ref[b]
            rinv_in[b] = lax.rsqrt(jnp.mean(xb * xb, axis=-1, keepdims=True) + EPS)
        return run

    def in_norm(b):
        def run():
            hb_scr[_rows(b), :] = ((x_ref[b] * rinv_in[b]) * ng).astype(_bf16)
        return run

    causal = (lax.broadcasted_iota(jnp.int32, (TT, TT), 0)
              >= lax.broadcasted_iota(jnp.int32, (TT, TT), 1))
    z_bufs = (z0_scr, z1_scr)
    z_ready = set()

    def w_in_cols(group, c):
        start = group * D_MODEL + c * CB
        return w_in_ref[:, start:start + CB]

    def project_pieces(c):
        z_scr = z_bufs[c % 2]

        def piece(slot, r):
            def run():
                rs = slice(r * MH, (r + 1) * MH)
                z_scr[rs, _cols(slot)] = _dot(hb_scr[rs, :], w_in_cols(Z_GROUPS[slot], c))
                z_ready.add((c, slot, r))
            return run
        return [piece(slot, r) for slot in range(NZ) for r in range(M // MH)]

    def merge_gate_pieces(n):
        def piece(k, r):
            def run():
                rs = slice(r * MH, (r + 1) * MH)
                gm_scr[rs, _cols(2 * n + k)] = _dot(hb_scr[rs, :], w_in_cols((G_MA, G_MB)[k], n))
            return run
        return [piece(k, r) for k in range(2) for r in range(M // MH)]

    def channel_block_pieces(c):
        z_scr = z_bufs[c % 2]
        cs = _cols(c)

        def zcols(slot, b):
            assert (c, slot, b * TT // MH) in z_ready, "projection piece emitted after its reader"
            return z_scr[_rows(b), _cols(slot)]

        conv_b, bx, ba, lam, hg_g = (cvec_ref[k:k + 1, cs] for k in range(5))
        cw = cw_ref[:, cs]
        xa = [None] * NB
        kk = [None] * NB

        def conv(b):
            def run():
                xc_scr[b, 0:8, :] = tail_scr[c, b]
                xc_scr[b, 8:8 + TT, :] = zcols(Z_XA, b)
                acc = conv_b + cw[3:4] * xc_scr[b, 8:8 + TT, :]
                for k in range(CONV_WIDTH - 1):
                    acc = acc + cw[k:k + 1] * xc_scr[b, 5 + k:5 + k + TT, :]
                xa[b] = acc
                tail_scr[c, b] = xc_scr[b, TT:TT + 8, :]
            return run

        def gate_dot():
            gt_scr[...] = _dot(jnp.concatenate(xa, axis=0).astype(_bf16), wbd_ref[c])

        def gates(b):
            def run():
                neg_lam = -lam
                softplus = jnp.maximum(neg_lam, 0.0) + jnp.log(1.0 + jnp.exp(-jnp.abs(neg_lam)))
                gate_i = _sigmoid(gt_scr[_rows(b), :CB] + bx)
                gate_r = _sigmoid(gt_scr[_rows(b), CB:] + ba)
                a = jnp.exp(((-LRU_C) * softplus) * gate_r)
                mult = _sqrt_nonneg(1.0 - a * a)
                _to_pitched(a_scr, b, a)
                _to_pitched(u_scr, b, mult * gate_i * xa[b])
            return run

        def forget(b):
            def run():
                lbl = lbl_ref[:, cs]
                lmax = jnp.maximum(lbl[0:1], lbl[1:2])
                e0 = jnp.exp(lbl[0:1] - lmax)
                e1 = jnp.exp(lbl[1:2] - lmax)
                lb = e0 / (e0 + e1)
                f = lb + (1.0 - lb) * _sigmoid(zcols(Z_F, b))
                _to_pitched(lf_scr, b, jnp.log(f))
                kk[b] = 1.0 - f
            return run

        cums = [jnp.zeros((NB, LANES), _f32) for _ in range(LT)]
        hs = [None] * LT

        def cumsum_steps(t0):
            def run():
                for t in range(t0, t0 + SCAN_CHUNK):
                    rows = pl.ds(t, NB, stride=PITCH)
                    for l in range(LT):
                        cums[l] = cums[l] + lf_scr[l, rows, :]
                        bc_scr[l, rows, :] = cums[l]
            return run

        def lru_steps(t0):
            def run():
                if t0 == 0:
                    h0 = hcar_scr[c]
                    for l in range(LT):
                        hs[l] = h0[:, l * LANES:(l + 1) * LANES]
                for t in range(t0, t0 + SCAN_CHUNK):
                    rows = pl.ds(t, NB, stride=PITCH)
                    for l in range(LT):
                        hs[l] = a_scr[l, rows, :] * hs[l] + u_scr[l, rows, :]
                        h_scr[l, rows, :] = hs[l]
                if t0 + SCAN_CHUNK == TT:
                    hcar_scr[c] = jnp.concatenate(hs, axis=1)
            return run

        def gated_a(b):
            def run():
                ya_scr[_rows(b), cs] = (_from_pitched(h_scr, b) * _silu(zcols(Z_GA, b))).astype(_bf16)
            return run

        att_in = [None] * NB
        att_w = {}
        o_raw = {}
        heads = [(b, hh) for b in range(NB) for hh in range(HPB)]

        def head_cols(hh):
            return slice(hh * HEAD_DIM, (hh + 1) * HEAD_DIM)

        def att_prologue(b):
            def run():
                bc = _from_pitched(bc_scr, b)
                b_mid = bc[CHUNK // 2:CHUNK // 2 + 1]
                b_last = bc[CHUNK - 1:CHUNK]
                q_in = _silu(zcols(Z_Q, b)) * jnp.exp(bc - b_mid) * HG_SCALE
                k_in = kk[b] * jnp.exp(b_mid - bc)
                q_inter = (q_in * jnp.exp(b_mid)).astype(_bf16)
                k_state = (k_in * jnp.exp(b_last - b_mid)).astype(_bf16)
                att_in[b] = (q_in.astype(_bf16), k_in.astype(_bf16), q_inter, k_state,
                             zcols(Z_I, b).astype(_bf16), jnp.exp(b_last))
            return run

        def att_scores(b, hh):
            def run():
                q_in, k_in = att_in[b][0], att_in[b][1]
                ls = head_cols(hh)
                att_w[b, hh] = jnp.where(causal, _dot_nt(q_in[:, ls], k_in[:, ls]), 0.0).astype(_bf16)
            return run

        def att_output(b, hh):
            def run():
                q_inter, v = att_in[b][2], att_in[b][4]
                ls = head_cols(hh)
                st = st_scr[c, b, hh]
                o_raw[b, hh] = _dot(att_w[b, hh], v[:, ls]) + _dot_nt(q_inter[:, ls], st.astype(_bf16))
            return run

        def state_update(b, hh):
            def run():
                k_state, v, decay = att_in[b][3], att_in[b][4], att_in[b][5]
                ls = head_cols(hh)
                st_scr[c, b, hh] = st_scr[c, b, hh] * decay[:, ls] + _dot_tn(v[:, ls], k_state[:, ls])
            return run

        def gated_b(b):
            def run():
                o_heads = []
                for hh in range(HPB):
                    o_h = o_raw[b, hh]
                    ms = jnp.mean(o_h * o_h, axis=-1, keepdims=True)
                    o_heads.append(o_h * lax.rsqrt(ms + EPS))
                o_n = jnp.concatenate(o_heads, axis=1) * hg_g
                yb_scr[_rows(b), cs] = (o_n * _silu(zcols(Z_GB, b))).astype(_bf16)
            return run

        steps = range(0, TT, SCAN_CHUNK)
        pieces = [(12, conv(b)) for b in range(NB)] + [(2, gate_dot)] + [(9, forget(b)) for b in range(NB)]
        for b, t0 in zip(range(NB), steps):
            pieces += [(22, gates(b)), (3, cumsum_steps(t0))]
        for b, t0 in zip(range(NB), steps):
            pieces += [(20, att_prologue(b)), (4, lru_steps(t0))]
        pieces += [(1, att_scores(b, hh)) for b, hh in heads]
        pieces += [(7, gated_a(b)) for b in range(NB // 2)]
        pieces += [(2, att_output(b, hh)) for b, hh in heads]
        pieces += [(7, gated_a(b)) for b in range(NB // 2, NB)]
        pieces += [(3, state_update(b, hh)) for b, hh in heads]
        pieces += [(14, gated_b(b)) for b in range(NB)]
        return pieces

    def branch_partial_pieces():
        ks = slice(0, KH)

        def piece(src, w_ref, acc, n, r):
            def run():
                rs = slice(r * MH, (r + 1) * MH)
                acc[rs, _cols(n)] = _dot(src[rs, ks], w_ref[ks, _cols(n)])
            return run
        return [piece(src, w_ref, acc, n, r)
                for n in range(NCB) for r in range(M // MH)
                for src, w_ref, acc in ((ya_scr, pa_ref, acc_a_scr), (yb_scr, pb_ref, acc_b_scr))]

    head_z = project_pieces(0)
    first = [p for i, p in enumerate(head_z) if i // (M // MH) in (Z_XA, Z_F)]
    rest = [p for i, p in enumerate(head_z) if i // (M // MH) not in (Z_XA, Z_F)]
    for b in range(NB):
        in_scale(b)()
    for b in range(NB // 2):
        in_norm(b)()
    _emit_interleaved([(1, in_norm(b)) for b in range(NB // 2, NB)], first[0::2])
    for piece in first[1::2]:
        piece()

    for c in range(NCB):
        filler = rest if c == 0 else []
        if c + 1 < NCB:
            filler = filler + project_pieces(c + 1)
        filler = filler + merge_gate_pieces(c)
        if c == NCB - 1:
            filler = filler + branch_partial_pieces()
        _emit_interleaved(channel_block_pieces(c), filler)

    ssq = [jnp.zeros((TT, 1), _f32) for _ in range(NB)]
    fg = fg_ref[...]
    k2 = slice(KH, D_MODEL)

    def merge(r, n):
        def run():
            rs = slice(r * MH, (r + 1) * MH)
            cs = _cols(n)
            out_a = acc_a_scr[rs, cs] + _dot(ya_scr[rs, k2], pa_ref[k2, cs])
            out_b = acc_b_scr[rs, cs] + _dot(yb_scr[rs, k2], pb_ref[k2, cs])
            g_a = _sigmoid(gm_scr[rs, _cols(2 * n)] + bm_ref[0:1, cs])
            g_b = _sigmoid(gm_scr[rs, _cols(2 * n + 1)] + bm_ref[1:2, cs])
            mix_scr[rs, cs] = (g_a * out_a + g_b * out_b).astype(_bf16)
        return run

    def out_proj(r, n):
        def run():
            rs = slice(r * MH, (r + 1) * MH)
            cs = _cols(n)
            proj = _dot(mix_scr[rs, :], wo_ref[:, cs])
            for i in range(MH // TT):
                b = r * (MH // TT) + i
                res = x_ref[b, :, cs] + proj[_rows(i)]
                o_ref[b, :, cs] = res
                ssq[b] = ssq[b] + jnp.sum(res * res, axis=-1, keepdims=True)
        return run

    def out_norm(b):
        def run():
            o_ref[b] = o_ref[b] * lax.rsqrt(ssq[b] * (1.0 / D_MODEL) + EPS) * fg
        return run

    halves = range(M // MH)
    bph = MH // TT
    for r in halves:
        stage = [(1, merge(r, n)) for n in range(NCB)]
        before = [out_proj(r - 1, n) for n in range(NCB)] if r > 0 else []
        _emit_interleaved(stage, before)
        if r > 1:
            for b in range((r - 2) * bph, (r - 1) * bph):
                out_norm(b)()
    last = halves[-1]
    _emit_interleaved([(1, out_proj(last, n)) for n in range(NCB)],
                      [out_norm(b) for b in range((last - 1) * bph, last * bph)])
    for b in range(last * bph, NB):
        out_norm(b)()


def _resident(shape):
    zeros = (0,) * len(shape)
    return pl.BlockSpec(shape, lambda i, j: zeros, pipeline_mode=pl.Buffered(1))


@jax.jit
def kernel(x, w_in, b_merge, conv_w, conv_b, rg_wx, rg_bx, rg_wa, rg_ba, rg_lambda,
           hg_lb_logits, hg_norm_g, proj_a, proj_b, w_out, norm_g, final_norm_g):
    bsz, seq, d = x.shape
    assert d == D_MODEL and bsz % NB == 0 and seq % TT == 0
    assert w_in.shape[0] == 1, "single-layer block"

    w_in_b = w_in[0].astype(_bf16)
    lpb = CB // LRU_BLOCK_DIM
    wbd = jnp.zeros((NCB, CB, 2 * CB), _bf16)
    for c in range(NCB):
        for k in range(lpb):
            r = slice(k * LRU_BLOCK_DIM, (k + 1) * LRU_BLOCK_DIM)
            wbd = wbd.at[c, r, r].set(rg_wx[0, c * lpb + k].astype(_bf16))
            wbd = wbd.at[c, r, CB + k * LRU_BLOCK_DIM:CB + (k + 1) * LRU_BLOCK_DIM].set(
                rg_wa[0, c * lpb + k].astype(_bf16))
    pa = proj_a[0].astype(_bf16)
    pb = proj_b[0].astype(_bf16)
    wo = w_out[0].astype(_bf16)

    zeros = jnp.zeros((D_MODEL,), _f32)
    cvec = jnp.stack([conv_b[0], rg_bx[0].reshape(-1), rg_ba[0].reshape(-1), rg_lambda[0],
                      jnp.tile(hg_norm_g[0], D_MODEL // HEAD_DIM), zeros, zeros, zeros])
    cw = conv_w[0]
    lbl = hg_lb_logits
    bm = b_merge[0].reshape(2, D_MODEL)
    ng = norm_g[0].reshape(1, D_MODEL)
    fg = final_norm_g.reshape(1, D_MODEL)

    x_spec = pl.BlockSpec((NB, TT, D_MODEL), lambda i, j: (i, j, 0))
    operands = (w_in_b, wbd, pa, pb, wo, cw, cvec, lbl, bm, ng, fg)
    pitched = pltpu.VMEM((LT, NB * PITCH, LANES), _f32)
    return pl.pallas_call(
        _block_kernel,
        grid=(bsz // NB, seq // TT),
        in_specs=[x_spec] + [_resident(op.shape) for op in operands],
        out_specs=x_spec,
        out_shape=jax.ShapeDtypeStruct(x.shape, x.dtype),
        scratch_shapes=[
            pltpu.VMEM((M, D_MODEL), _bf16),
            pltpu.VMEM((M, NZ * CB), _f32),
            pltpu.VMEM((M, NZ * CB), _f32),
            pltpu.VMEM((M, 2 * D_MODEL), _f32),
            pltpu.VMEM((M, 2 * CB), _f32),
            pltpu.VMEM((NB, PITCH, CB), _f32),
            pltpu.VMEM((NCB, NB, 8, CB), _f32),
            pitched, pitched, pitched, pitched, pitched,
            pltpu.VMEM((NCB, NB, CB), _f32),
            pltpu.VMEM((NCB, NB, HPB, HEAD_DIM, HEAD_DIM), _f32),
            pltpu.VMEM((M, D_MODEL), _bf16),
            pltpu.VMEM((M, D_MODEL), _bf16),
            pltpu.VMEM((M, D_MODEL), _bf16),
            pltpu.VMEM((M, D_MODEL), _f32),
            pltpu.VMEM((M, D_MODEL), _f32),
        ],
        compiler_params=pltpu.CompilerParams(
            dimension_semantics=("arbitrary", "arbitrary"),
            vmem_limit_bytes=VMEM_LIMIT_BYTES),
        name="hawk_hgrn2_block",
    )(x, *operands)
```

```python
import jax
import jax.numpy as jnp
from jax import lax
from jax.experimental import pallas as pl
from jax.experimental.pallas import tpu as pltpu

D_MODEL = 1024
LRU_BLOCK_DIM = 128
CONV_WIDTH = 4
LRU_C = 8.0
HEAD_DIM = 128
CHUNK = 64
HG_SCALE = HEAD_DIM ** -0.5
EPS = 1e-6

NB = 8
TT = CHUNK
M = NB * TT
MH = M // 2
KH = D_MODEL // 2
CB = 256
NCB = D_MODEL // CB
HPB = CB // HEAD_DIM
LANES = 128
LT = CB // LANES
PITCH = TT + 8
SCAN_GROUPS = 4
GS = TT // SCAN_GROUPS
SCAN_CHUNK = 2
VMEM_LIMIT_BYTES = 58 * 1024 * 1024

G_XA, G_GA, G_Q, G_F, G_I, G_GB, G_MA, G_MB = range(8)
Z_GROUPS = (G_XA, G_GA, G_Q, G_F, G_I, G_GB)
Z_XA, Z_GA, Z_Q, Z_F, Z_I, Z_GB = range(6)
NZ = len(Z_GROUPS)

_f32 = jnp.float32
_bf16 = jnp.bfloat16


def _dot(a, b):
    return jnp.dot(a, b, preferred_element_type=_f32)


def _dot_nt(a, b):
    return lax.dot_general(a, b, (((1,), (1,)), ((), ())), preferred_element_type=_f32)


def _dot_tn(a, b):
    return lax.dot_general(a, b, (((0,), (0,)), ((), ())), preferred_element_type=_f32)


def _sigmoid(x):
    return 0.5 * jnp.tanh(0.5 * x) + 0.5


def _silu(x):
    h = 0.5 * x
    return h * jnp.tanh(h) + h


def _sqrt_nonneg(x):
    return jnp.where(x > 0.0, x * lax.rsqrt(x), 0.0)


def _rows(b):
    return slice(b * TT, (b + 1) * TT)


def _cols(n):
    return slice(n * CB, (n + 1) * CB)


def _to_pitched(scr, b, val):
    for l in range(LT):
        scr[l, b * PITCH:b * PITCH + TT, :] = val[:, l * LANES:(l + 1) * LANES]


def _from_pitched(scr, b):
    return jnp.concatenate([scr[l, b * PITCH:b * PITCH + TT, :] for l in range(LT)], axis=1)


def _emit_interleaved(primary, secondary):
    total = sum(w for w, _ in primary)
    done, spent = 0, 0
    for w, piece in primary:
        piece()
        spent += w
        target = spent * len(secondary) // total
        while done < target:
            secondary[done]()
            done += 1
    for piece in secondary[done:]:
        piece()


def _block_kernel(x_ref, w_in_ref, wbd_ref, pa_ref, pb_ref, wo_ref,
                  cw_ref, cvec_ref, lbl_ref, bm_ref, ng_ref, fg_ref,
                  o_ref,
                  hb_scr, z0_scr, z1_scr, gm_scr, gt_scr, xc_scr, tail_scr,
                  a_scr, u_scr, h_scr, p_scr, lf_scr, bc_scr, hst_scr, cst_scr,
                  hcar_scr, st_scr, ya_scr, yb_scr, mix_scr, acc_a_scr, acc_b_scr):
    j = pl.program_id(1)

    @pl.when(j == 0)
    def _reset_carries():
        tail_scr[...] = jnp.zeros_like(tail_scr)
        hcar_scr[...] = jnp.zeros_like(hcar_scr)
        st_scr[...] = jnp.zeros_like(st_scr)

    ng = ng_ref[...]
    rinv_in = [None] * NB

    def in_scale(b):
        def run():
            xb = x_ref[b]
            rinv_in[b] = lax.rsqrt(jnp.mean(xb * xb, axis=-1, keepdims=True) + EPS)
        return run

    def in_norm(b):
        def run():
            hb_scr[_rows(b), :] = ((x_ref[b] * rinv_in[b]) * ng).astype(_bf16)
        return run

    causal = (lax.broadcasted_iota(jnp.int32, (TT, TT), 0)
              >= lax.broadcasted_iota(jnp.int32, (TT, TT), 1))
    z_bufs = (z0_scr, z1_scr)
    z_ready = set()

    def w_in_cols(group, c):
        start = group * D_MODEL + c * CB
        return w_in_ref[:, start:start + CB]

    def project_pieces(c):
        z_scr = z_bufs[c % 2]

        def piece(slot, r):
            def run():
                rs = slice(r * MH, (r + 1) * MH)
                z_scr[rs, _cols(slot)] = _dot(hb_scr[rs, :], w_in_cols(Z_GROUPS[slot], c))
                z_ready.add((c, slot, r))
            return run
        return [piece(slot, r) for slot in range(NZ) for r in range(M // MH)]

    def merge_gate_pieces(n):
        def piece(k, r):
            def run():
                rs = slice(r * MH, (r + 1) * MH)
                gm_scr[rs, _cols(2 * n + k)] = _dot(hb_scr[rs, :], w_in_cols((G_MA, G_MB)[k], n))
            return run
        return [piece(k, r) for k in range(2) for r in range(M // MH)]

    def channel_block_pieces(c):
        z_scr = z_bufs[c % 2]
        cs = _cols(c)

        def zcols(slot, b):
            assert (c, slot, b * TT // MH) in z_ready, "projection piece emitted after its reader"
            return z_scr[_rows(b), _cols(slot)]

        conv_b, bx, ba, lam, hg_g = (cvec_ref[k:k + 1, cs] for k in range(5))
        cw = cw_ref[:, cs]
        xa = [None] * NB
        kk = [None] * NB

        def conv(b):
            def run():
                xc_scr[b, 0:8, :] = tail_scr[c, b]
                xc_scr[b, 8:8 + TT, :] = zcols(Z_XA, b)
                acc = conv_b + cw[3:4] * xc_scr[b, 8:8 + TT, :]
                for k in range(CONV_WIDTH - 1):
                    acc = acc + cw[k:k + 1] * xc_scr[b, 5 + k:5 + k + TT, :]
                xa[b] = acc
                tail_scr[c, b] = xc_scr[b, TT:TT + 8, :]
            return run

        def gate_dot():
            gt_scr[...] = _dot(jnp.concatenate(xa, axis=0).astype(_bf16), wbd_ref[c])

        def gates(b):
            def run():
                neg_lam = -lam
                softplus = jnp.maximum(neg_lam, 0.0) + jnp.log(1.0 + jnp.exp(-jnp.abs(neg_lam)))
                gate_i = _sigmoid(gt_scr[_rows(b), :CB] + bx)
                gate_r = _sigmoid(gt_scr[_rows(b), CB:] + ba)
                a = jnp.exp(((-LRU_C) * softplus) * gate_r)
                mult = _sqrt_nonneg(1.0 - a * a)
                _to_pitched(a_scr, b, a)
                _to_pitched(u_scr, b, mult * gate_i * xa[b])
            return run

        def forget(b):
            def run():
                lbl = lbl_ref[:, cs]
                lmax = jnp.maximum(lbl[0:1], lbl[1:2])
                e0 = jnp.exp(lbl[0:1] - lmax)
                e1 = jnp.exp(lbl[1:2] - lmax)
                lb = e0 / (e0 + e1)
                f = lb + (1.0 - lb) * _sigmoid(zcols(Z_F, b))
                _to_pitched(lf_scr, b, jnp.log(f))
                kk[b] = 1.0 - f
            return run

        cums = [[jnp.zeros((NB, LANES), _f32) for _ in range(LT)] for _ in range(SCAN_GROUPS)]
        hs = [[jnp.zeros((NB, LANES), _f32) for _ in range(LT)] for _ in range(SCAN_GROUPS)]
        ps = [[jnp.ones((NB, LANES), _f32) for _ in range(LT)] for _ in range(SCAN_GROUPS)]

        def cumsum_steps(s0):
            def run():
                for step in range(s0, s0 + SCAN_CHUNK):
                    for g in range(SCAN_GROUPS):
                        rows = pl.ds(g * GS + step, NB, stride=PITCH)
                        for l in range(LT):
                            cums[g][l] = cums[g][l] + lf_scr[l, rows, :]
                            bc_scr[l, rows, :] = cums[g][l]
            return run

        def cumsum_finalize():
            start = [jnp.zeros((NB, LANES), _f32) for _ in range(LT)]
            for g in range(SCAN_GROUPS):
                cst_scr[g] = jnp.concatenate(start, axis=1)
                start = [start[l] + cums[g][l] for l in range(LT)]

        def lru_steps(s0):
            def run():
                for step in range(s0, s0 + SCAN_CHUNK):
                    for g in range(SCAN_GROUPS):
                        rows = pl.ds(g * GS + step, NB, stride=PITCH)
                        for l in range(LT):
                            a_t = a_scr[l, rows, :]
                            hs[g][l] = a_t * hs[g][l] + u_scr[l, rows, :]
                            ps[g][l] = a_t * ps[g][l]
                            h_scr[l, rows, :] = hs[g][l]
                            p_scr[l, rows, :] = ps[g][l]
            return run

        def lru_finalize():
            h0 = hcar_scr[c]
            start = [h0[:, l * LANES:(l + 1) * LANES] for l in range(LT)]
            for g in range(SCAN_GROUPS):
                hst_scr[g] = jnp.concatenate(start, axis=1)
                start = [hs[g][l] + ps[g][l] * start[l] for l in range(LT)]
            hcar_scr[c] = jnp.concatenate(start, axis=1)

        def with_group_start(local, factor, start_scr, b):
            parts = []
            for g in range(SCAN_GROUPS):
                ts = slice(g * GS, (g + 1) * GS)
                start = start_scr[g, b:b + 1, :]
                parts.append(local[ts] + (start if factor is None else factor[ts] * start))
            return jnp.concatenate(parts, axis=0)

        def gated_a(b):
            def run():
                h = with_group_start(_from_pitched(h_scr, b), _from_pitched(p_scr, b), hst_scr, b)
                ya_scr[_rows(b), cs] = (h * _silu(zcols(Z_GA, b))).astype(_bf16)
            return run

        att_in = [None] * NB
        att_w = {}
        o_raw = {}
        heads = [(b, hh) for b in range(NB) for hh in range(HPB)]

        def head_cols(hh):
            return slice(hh * HEAD_DIM, (hh + 1) * HEAD_DIM)

        def att_prologue(b):
            def run():
                bc = with_group_start(_from_pitched(bc_scr, b), None, cst_scr, b)
                b_mid = bc[CHUNK // 2:CHUNK // 2 + 1]
                b_last = bc[CHUNK - 1:CHUNK]
                q_in = _silu(zcols(Z_Q, b)) * jnp.exp(bc - b_mid) * HG_SCALE
                k_in = kk[b] * jnp.exp(b_mid - bc)
                q_inter = (q_in * jnp.exp(b_mid)).astype(_bf16)
                k_state = (k_in * jnp.exp(b_last - b_mid)).astype(_bf16)
                att_in[b] = (q_in.astype(_bf16), k_in.astype(_bf16), q_inter, k_state,
                             zcols(Z_I, b).astype(_bf16), jnp.exp(b_last))
            return run

        def att_scores(b, hh):
            def run():
                q_in, k_in = att_in[b][0], att_in[b][1]
                ls = head_cols(hh)
                att_w[b, hh] = jnp.where(causal, _dot_nt(q_in[:, ls], k_in[:, ls]), 0.0).astype(_bf16)
            return run

        def att_output(b, hh):
            def run():
                q_inter, v = att_in[b][2], att_in[b][4]
                ls = head_cols(hh)
                st = st_scr[c, b, hh]
                o_raw[b, hh] = _dot(att_w[b, hh], v[:, ls]) + _dot_nt(q_inter[:, ls], st.astype(_bf16))
            return run

        def state_update(b, hh):
            def run():
                k_state, v, decay = att_in[b][3], att_in[b][4], att_in[b][5]
                ls = head_cols(hh)
                st_scr[c, b, hh] = st_scr[c, b, hh] * decay[:, ls] + _dot_tn(v[:, ls], k_state[:, ls])
            return run

        def gated_b(b):
            def run():
                o_heads = []
                for hh in range(HPB):
                    o_h = o_raw[b, hh]
                    ms = jnp.mean(o_h * o_h, axis=-1, keepdims=True)
                    o_heads.append(o_h * lax.rsqrt(ms + EPS))
                o_n = jnp.concatenate(o_heads, axis=1) * hg_g
                yb_scr[_rows(b), cs] = (o_n * _silu(zcols(Z_GB, b))).astype(_bf16)
            return run

        steps = range(0, GS, SCAN_CHUNK)
        pieces = [(12, conv(b)) for b in range(NB)] + [(2, gate_dot)] + [(9, forget(b)) for b in range(NB)]
        for b, s0 in zip(range(NB), steps):
            pieces += [(22, gates(b)), (3, cumsum_steps(s0))]
        pieces += [(1, cumsum_finalize)]
        for b, s0 in zip(range(NB), steps):
            pieces += [(21, att_prologue(b)), (6, lru_steps(s0))]
        pieces += [(1, lru_finalize)]
        pieces += [(1, att_scores(b, hh)) for b, hh in heads]
        pieces += [(9, gated_a(b)) for b in range(NB // 2)]
        pieces += [(2, att_output(b, hh)) for b, hh in heads]
        pieces += [(9, gated_a(b)) for b in range(NB // 2, NB)]
        pieces += [(3, state_update(b, hh)) for b, hh in heads]
        pieces += [(14, gated_b(b)) for b in range(NB)]
        return pieces

    def branch_partial_pieces():
        ks = slice(0, KH)

        def piece(src, w_ref, acc, n, r):
            def run():
                rs = slice(r * MH, (r + 1) * MH)
                acc[rs, _cols(n)] = _dot(src[rs, ks], w_ref[ks, _cols(n)])
            return run
        return [piece(src, w_ref, acc, n, r)
                for n in range(NCB) for r in range(M // MH)
                for src, w_ref, acc in ((ya_scr, pa_ref, acc_a_scr), (yb_scr, pb_ref, acc_b_scr))]

    head_z = project_pieces(0)
    first = [p for i, p in enumerate(head_z) if i // (M // MH) in (Z_XA, Z_F)]
    rest = [p for i, p in enumerate(head_z) if i // (M // MH) not in (Z_XA, Z_F)]
    for b in range(NB):
        in_scale(b)()
    for b in range(NB // 2):
        in_norm(b)()
    _emit_interleaved([(1, in_norm(b)) for b in range(NB // 2, NB)], first[0::2])
    for piece in first[1::2]:
        piece()

    for c in range(NCB):
        filler = rest if c == 0 else []
        if c + 1 < NCB:
            filler = filler + project_pieces(c + 1)
        filler = filler + merge_gate_pieces(c)
        if c == NCB - 1:
            filler = filler + branch_partial_pieces()
        _emit_interleaved(channel_block_pieces(c), filler)

    ssq = [jnp.zeros((TT, 1), _f32) for _ in range(NB)]
    fg = fg_ref[...]
    k2 = slice(KH, D_MODEL)

    def merge(r, n):
        def run():
            rs = slice(r * MH, (r + 1) * MH)
            cs = _cols(n)
            out_a = acc_a_scr[rs, cs] + _dot(ya_scr[rs, k2], pa_ref[k2, cs])
            out_b = acc_b_scr[rs, cs] + _dot(yb_scr[rs, k2], pb_ref[k2, cs])
            g_a = _sigmoid(gm_scr[rs, _cols(2 * n)] + bm_ref[0:1, cs])
            g_b = _sigmoid(gm_scr[rs, _cols(2 * n + 1)] + bm_ref[1:2, cs])
            mix_scr[rs, cs] = (g_a * out_a + g_b * out_b).astype(_bf16)
        return run

    def out_proj(r, n):
        def run():
            rs = slice(r * MH, (r + 1) * MH)
            cs = _cols(n)
            proj = _dot(mix_scr[rs, :], wo_ref[:, cs])
            for i in range(MH // TT):
                b = r * (MH // TT) + i
                res = x_ref[b, :, cs] + proj[_rows(i)]
                o_ref[b, :, cs] = res
                ssq[b] = ssq[b] + jnp.sum(res * res, axis=-1, keepdims=True)
        return run

    def out_norm(b):
        def run():
            o_ref[b] = o_ref[b] * lax.rsqrt(ssq[b] * (1.0 / D_MODEL) + EPS) * fg
        return run

    halves = range(M // MH)
    bph = MH // TT
    for r in halves:
        stage = [(1, merge(r, n)) for n in range(NCB)]
        before = [out_proj(r - 1, n) for n in range(NCB)] if r > 0 else []
        _emit_interleaved(stage, before)
        if r > 1:
            for b in range((r - 2) * bph, (r - 1) * bph):
                out_norm(b)()
    last = halves[-1]
    _emit_interleaved([(1, out_proj(last, n)) for n in range(NCB)],
                      [out_norm(b) for b in range((last - 1) * bph, last * bph)])
    for b in range(last * bph, NB):
        out_norm(b)()


def _resident(shape):
    zeros = (0,) * len(shape)
    return pl.BlockSpec(shape, lambda i, j: zeros, pipeline_mode=pl.Buffered(1))


@jax.jit
def kernel(x, w_in, b_merge, conv_w, conv_b, rg_wx, rg_bx, rg_wa, rg_ba, rg_lambda,
           hg_lb_logits, hg_norm_g, proj_a, proj_b, w_out, norm_g, final_norm_g):
    bsz, seq, d = x.shape
    assert d == D_MODEL and bsz % NB == 0 and seq % TT == 0
    assert w_in.shape[0] == 1, "single-layer block"

    w_in_b = w_in[0].astype(_bf16)
    lpb = CB // LRU_BLOCK_DIM
    wbd = jnp.zeros((NCB, CB, 2 * CB), _bf16)
    for c in range(NCB):
        for k in range(lpb):
            r = slice(k * LRU_BLOCK_DIM, (k + 1) * LRU_BLOCK_DIM)
            wbd = wbd.at[c, r, r].set(rg_wx[0, c * lpb + k].astype(_bf16))
            wbd = wbd.at[c, r, CB + k * LRU_BLOCK_DIM:CB + (k + 1) * LRU_BLOCK_DIM].set(
                rg_wa[0, c * lpb + k].astype(_bf16))
    pa = proj_a[0].astype(_bf16)
    pb = proj_b[0].astype(_bf16)
    wo = w_out[0].astype(_bf16)

    zeros = jnp.zeros((D_MODEL,), _f32)
    cvec = jnp.stack([conv_b[0], rg_bx[0].reshape(-1), rg_ba[0].reshape(-1), rg_lambda[0],
                      jnp.tile(hg_norm_g[0], D_MODEL // HEAD_DIM), zeros, zeros, zeros])
    cw = conv_w[0]
    lbl = hg_lb_logits
    bm = b_merge[0].reshape(2, D_MODEL)
    ng = norm_g[0].reshape(1, D_MODEL)
    fg = final_norm_g.reshape(1, D_MODEL)

    x_spec = pl.BlockSpec((NB, TT, D_MODEL), lambda i, j: (i, j, 0))
    operands = (w_in_b, wbd, pa, pb, wo, cw, cvec, lbl, bm, ng, fg)
    pitched = pltpu.VMEM((LT, NB * PITCH, LANES), _f32)
    return pl.pallas_call(
        _block_kernel,
        grid=(bsz // NB, seq // TT),
        in_specs=[x_spec] + [_resident(op.shape) for op in operands],
        out_specs=x_spec,
        out_shape=jax.ShapeDtypeStruct(x.shape, x.dtype),
        scratch_shapes=[
            pltpu.VMEM((M, D_MODEL), _bf16),
            pltpu.VMEM((M, NZ * CB), _f32),
            pltpu.VMEM((M, NZ * CB), _f32),
            pltpu.VMEM((M, 2 * D_MODEL), _f32),
            pltpu.VMEM((M, 2 * CB), _f32),
            pltpu.VMEM((NB, PITCH, CB), _f32),
            pltpu.VMEM((NCB, NB, 8, CB), _f32),
            pitched, pitched, pitched, pitched, pitched, pitched,
            pltpu.VMEM((SCAN_GROUPS, NB, CB), _f32),
            pltpu.VMEM((SCAN_GROUPS, NB, CB), _f32),
            pltpu.VMEM((NCB, NB, CB), _f32),
            pltpu.VMEM((NCB, NB, HPB, HEAD_DIM, HEAD_DIM), _f32),
            pltpu.VMEM((M, D_MODEL), _bf16),
            pltpu.VMEM((M, D_MODEL), _bf16),
            pltpu.VMEM((M, D_MODEL), _bf16),
            pltpu.VMEM((M, D_MODEL), _f32),
            pltpu.VMEM((M, D_MODEL), _f32),
        ],
        compiler_params=pltpu.CompilerParams(
            dimension_semantics=("arbitrary", "arbitrary"),
            vmem_limit_bytes=VMEM_LIMIT_BYTES),
        name="hawk_hgrn2_block",
    )(x, *operands)
```

```python
import jax
import jax.numpy as jnp
from jax import lax
from jax.experimental import pallas as pl
from jax.experimental.pallas import tpu as pltpu

D_MODEL = 1024
LRU_BLOCK_DIM = 128
CONV_WIDTH = 4
LRU_C = 8.0
HEAD_DIM = 128
CHUNK = 64
HG_SCALE = HEAD_DIM ** -0.5
EPS = 1e-6

NB = 8
TT = CHUNK
M = NB * TT
MH = M // 2
KH = D_MODEL // 2
CB = 256
NCB = D_MODEL // CB
HPB = CB // HEAD_DIM
LANES = 128
LT = CB // LANES
PITCH = TT + 8
SCAN_CHUNK = 8
VMEM_LIMIT_BYTES = 58 * 1024 * 1024

G_XA, G_GA, G_Q, G_F, G_I, G_GB, G_MA, G_MB = range(8)
Z_GROUPS = (G_XA, G_GA, G_Q, G_F, G_I, G_GB)
Z_XA, Z_GA, Z_Q, Z_F, Z_I, Z_GB, Z_KK = range(7)
NZ = len(Z_GROUPS)
NZ_SLOTS = NZ + 1

_f32 = jnp.float32
_bf16 = jnp.bfloat16


def _dot(a, b):
    return jnp.dot(a, b, preferred_element_type=_f32)


def _dot_nt(a, b):
    return lax.dot_general(a, b, (((1,), (1,)), ((), ())), preferred_element_type=_f32)


def _dot_tn(a, b):
    return lax.dot_general(a, b, (((0,), (0,)), ((), ())), preferred_element_type=_f32)


def _sigmoid(x):
    return 0.5 * jnp.tanh(0.5 * x) + 0.5


def _silu(x):
    h = 0.5 * x
    return h * jnp.tanh(h) + h


def _sqrt_nonneg(x):
    return jnp.where(x > 0.0, x * lax.rsqrt(x), 0.0)


def _rows(b):
    return slice(b * TT, (b + 1) * TT)


def _cols(n):
    return slice(n * CB, (n + 1) * CB)


def _to_pitched(scr, b, val):
    for l in range(LT):
        scr[l, b * PITCH:b * PITCH + TT, :] = val[:, l * LANES:(l + 1) * LANES]


def _from_pitched(scr, b):
    return jnp.concatenate([scr[l, b * PITCH:b * PITCH + TT, :] for l in range(LT)], axis=1)


def _emit_interleaved(primary, secondary):
    total = sum(w for w, _ in primary)
    done, spent = 0, 0
    for w, piece in primary:
        piece()
        spent += w
        target = spent * len(secondary) // total
        while done < target:
            secondary[done]()
            done += 1
    for piece in secondary[done:]:
        piece()


def _block_kernel(x_ref, w_in_ref, wbd_ref, pa_ref, pb_ref, wo_ref,
                  cw_ref, cvec_ref, lbl_ref, bm_ref, ng_ref, fg_ref,
                  o_ref,
                  hb_scr, z0_scr, z1_scr, gm_scr, gt_scr, xc_scr, tail_scr,
                  a_scr, u_scr, h_scr, lf_scr, bc_scr,
                  hcar_scr, st_scr, ya_scr, yb_scr, mix_scr, acc_a_scr, acc_b_scr):
    j = pl.program_id(1)

    @pl.when(j == 0)
    def _reset_carries():
        tail_scr[...] = jnp.zeros_like(tail_scr)
        hcar_scr[...] = jnp.zeros_like(hcar_scr)
        st_scr[...] = jnp.zeros_like(st_scr)

    ng = ng_ref[...]
    rinv_in = [None] * NB

    def in_scale(b):
        def run():
            xb = x_ref[b]
            rinv_in[b] = lax.rsqrt(jnp.mean(xb * xb, axis=-1, keepdims=True) + EPS)
        return run

    def in_norm(b):
        def run():
            hb_scr[_rows(b), :] = ((x_ref[b] * rinv_in[b]) * ng).astype(_bf16)
        return run

    causal = (lax.broadcasted_iota(jnp.int32, (TT, TT), 0)
              >= lax.broadcasted_iota(jnp.int32, (TT, TT), 1))
    z_bufs = (z0_scr, z1_scr)
    z_ready = set()

    def w_in_cols(group, c):
        start = group * D_MODEL + c * CB
        return w_in_ref[:, start:start + CB]

    def project_pieces(c):
        z_scr = z_bufs[c % 2]

        def piece(slot, r):
            def run():
                rs = slice(r * MH, (r + 1) * MH)
                raw = _dot(hb_scr[rs, :], w_in_cols(Z_GROUPS[slot], c))
                if slot in (Z_GA, Z_Q, Z_GB):
                    z_scr[rs, _cols(slot)] = _silu(raw)
                elif slot == Z_F:
                    lbl = lbl_ref[:, _cols(c)]
                    lmax = jnp.maximum(lbl[0:1], lbl[1:2])
                    e0 = jnp.exp(lbl[0:1] - lmax)
                    e1 = jnp.exp(lbl[1:2] - lmax)
                    lb = e0 / (e0 + e1)
                    f = lb + (1.0 - lb) * _sigmoid(raw)
                    z_scr[rs, _cols(Z_F)] = jnp.log(f)
                    z_scr[rs, _cols(Z_KK)] = 1.0 - f
                    z_ready.add((c, Z_KK, r))
                else:
                    z_scr[rs, _cols(slot)] = raw
                z_ready.add((c, slot, r))
            return run
        return [piece(slot, r) for slot in range(NZ) for r in range(M // MH)]

    def merge_gate_pieces(n):
        def piece(k, r):
            def run():
                rs = slice(r * MH, (r + 1) * MH)
                gm_scr[rs, _cols(2 * n + k)] = _dot(hb_scr[rs, :], w_in_cols((G_MA, G_MB)[k], n))
            return run
        return [piece(k, r) for k in range(2) for r in range(M // MH)]

    def channel_block_pieces(c):
        z_scr = z_bufs[c % 2]
        cs = _cols(c)

        def zcols(slot, b):
            assert (c, slot, b * TT // MH) in z_ready, "projection piece emitted after its reader"
            return z_scr[_rows(b), _cols(slot)]

        conv_b, bx, ba, lam, hg_g = (cvec_ref[k:k + 1, cs] for k in range(5))
        cw = cw_ref[:, cs]
        xa = [None] * NB

        def conv(b):
            def run():
                xc_scr[b, 0:8, :] = tail_scr[c, b]
                xc_scr[b, 8:8 + TT, :] = zcols(Z_XA, b)
                acc = conv_b + cw[3:4] * xc_scr[b, 8:8 + TT, :]
                for k in range(CONV_WIDTH - 1):
                    acc = acc + cw[k:k + 1] * xc_scr[b, 5 + k:5 + k + TT, :]
                xa[b] = acc
                tail_scr[c, b] = xc_scr[b, TT:TT + 8, :]
            return run

        def gate_dot():
            gt_scr[...] = _dot(jnp.concatenate(xa, axis=0).astype(_bf16), wbd_ref[c])

        def gates(b):
            def run():
                neg_lam = -lam
                softplus = jnp.maximum(neg_lam, 0.0) + jnp.log(1.0 + jnp.exp(-jnp.abs(neg_lam)))
                gate_i = _sigmoid(gt_scr[_rows(b), :CB] + bx)
                gate_r = _sigmoid(gt_scr[_rows(b), CB:] + ba)
                a = jnp.exp(((-LRU_C) * softplus) * gate_r)
                mult = _sqrt_nonneg(1.0 - a * a)
                _to_pitched(a_scr, b, a)
                _to_pitched(u_scr, b, mult * gate_i * xa[b])
            return run

        def forget(b):
            def run():
                _to_pitched(lf_scr, b, zcols(Z_F, b))
            return run

        cums = [jnp.zeros((NB, LANES), _f32) for _ in range(LT)]
        hs = [None] * LT

        def cumsum_steps(t0):
            def run():
                for t in range(t0, t0 + SCAN_CHUNK):
                    rows = pl.ds(t, NB, stride=PITCH)
                    for l in range(LT):
                        cums[l] = cums[l] + lf_scr[l, rows, :]
                        bc_scr[l, rows, :] = cums[l]
            return run

        def lru_steps(t0):
            def run():
                if t0 == 0:
                    h0 = hcar_scr[c]
                    for l in range(LT):
                        hs[l] = h0[:, l * LANES:(l + 1) * LANES]
                for t in range(t0, t0 + SCAN_CHUNK):
                    rows = pl.ds(t, NB, stride=PITCH)
                    for l in range(LT):
                        hs[l] = a_scr[l, rows, :] * hs[l] + u_scr[l, rows, :]
                        h_scr[l, rows, :] = hs[l]
                if t0 + SCAN_CHUNK == TT:
                    hcar_scr[c] = jnp.concatenate(hs, axis=1)
            return run

        def gated_a(b):
            def run():
                ya_scr[_rows(b), cs] = (_from_pitched(h_scr, b) * zcols(Z_GA, b)).astype(_bf16)
            return run

        att_in = [None] * NB
        att_w = {}
        o_raw = {}
        heads = [(b, hh) for b in range(NB) for hh in range(HPB)]

        def head_cols(hh):
            return slice(hh * HEAD_DIM, (hh + 1) * HEAD_DIM)

        def att_prologue(b):
            def run():
                bc = _from_pitched(bc_scr, b)
                b_mid = bc[CHUNK // 2:CHUNK // 2 + 1]
                b_last = bc[CHUNK - 1:CHUNK]
                q_in = zcols(Z_Q, b) * jnp.exp(bc - b_mid) * HG_SCALE
                k_in = zcols(Z_KK, b) * jnp.exp(b_mid - bc)
                q_inter = (q_in * jnp.exp(b_mid)).astype(_bf16)
                k_state = (k_in * jnp.exp(b_last - b_mid)).astype(_bf16)
                att_in[b] = (q_in.astype(_bf16), k_in.astype(_bf16), q_inter, k_state,
                             zcols(Z_I, b).astype(_bf16), jnp.exp(b_last))
            return run

        def att_scores(b, hh):
            def run():
                q_in, k_in = att_in[b][0], att_in[b][1]
                ls = head_cols(hh)
                att_w[b, hh] = jnp.where(causal, _dot_nt(q_in[:, ls], k_in[:, ls]), 0.0).astype(_bf16)
            return run

        def att_output(b, hh):
            def run():
                q_inter, v = att_in[b][2], att_in[b][4]
                ls = head_cols(hh)
                st = st_scr[c, b, hh]
                o_raw[b, hh] = _dot(att_w[b, hh], v[:, ls]) + _dot_nt(q_inter[:, ls], st.astype(_bf16))
            return run

        def state_update(b, hh):
            def run():
                k_state, v, decay = att_in[b][3], att_in[b][4], att_in[b][5]
                ls = head_cols(hh)
                st_scr[c, b, hh] = st_scr[c, b, hh] * decay[:, ls] + _dot_tn(v[:, ls], k_state[:, ls])
            return run

        def gated_b(b):
            def run():
                o_heads = []
                for hh in range(HPB):
                    o_h = o_raw[b, hh]
                    ms = jnp.mean(o_h * o_h, axis=-1, keepdims=True)
                    o_heads.append(o_h * lax.rsqrt(ms + EPS))
                o_n = jnp.concatenate(o_heads, axis=1) * hg_g
                yb_scr[_rows(b), cs] = (o_n * zcols(Z_GB, b)).astype(_bf16)
            return run

        steps = range(0, TT, SCAN_CHUNK)
        pieces = [(12, conv(b)) for b in range(NB)] + [(2, gate_dot)] + [(2, forget(b)) for b in range(NB)]
        for b, t0 in zip(range(NB), steps):
            pieces += [(22, gates(b)), (3, cumsum_steps(t0))]
        for b, t0 in zip(range(NB), steps):
            pieces += [(16, att_prologue(b)), (4, lru_steps(t0))]
        pieces += [(1, att_scores(b, hh)) for b, hh in heads]
        pieces += [(3, gated_a(b)) for b in range(NB // 2)]
        pieces += [(2, att_output(b, hh)) for b, hh in heads]
        pieces += [(3, gated_a(b)) for b in range(NB // 2, NB)]
        pieces += [(3, state_update(b, hh)) for b, hh in heads]
        pieces += [(10, gated_b(b)) for b in range(NB)]
        return pieces

    def branch_partial_pieces():
        ks = slice(0, KH)

        def piece(src, w_ref, acc, n, r):
            def run():
                rs = slice(r * MH, (r + 1) * MH)
                acc[rs, _cols(n)] = _dot(src[rs, ks], w_ref[ks, _cols(n)])
            return run
        return [piece(src, w_ref, acc, n, r)
                for n in range(NCB) for r in range(M // MH)
                for src, w_ref, acc in ((ya_scr, pa_ref, acc_a_scr), (yb_scr, pb_ref, acc_b_scr))]

    head_z = project_pieces(0)
    first = [p for i, p in enumerate(head_z) if i // (M // MH) in (Z_XA, Z_F)]
    rest = [p for i, p in enumerate(head_z) if i // (M // MH) not in (Z_XA, Z_F)]
    for b in range(NB):
        in_scale(b)()
    for b in range(NB // 2):
        in_norm(b)()
    _emit_interleaved([(1, in_norm(b)) for b in range(NB // 2, NB)], first[0::2])
    for piece in first[1::2]:
        piece()

    for c in range(NCB):
        filler = rest if c == 0 else []
        if c + 1 < NCB:
            filler = filler + project_pieces(c + 1)
        filler = filler + merge_gate_pieces(c)
        if c == NCB - 1:
            filler = filler + branch_partial_pieces()
        _emit_interleaved(channel_block_pieces(c), filler)

    ssq = [jnp.zeros((TT, 1), _f32) for _ in range(NB)]
    fg = fg_ref[...]
    k2 = slice(KH, D_MODEL)

    def merge(r, n):
        def run():
            rs = slice(r * MH, (r + 1) * MH)
            cs = _cols(n)
            out_a = acc_a_scr[rs, cs] + _dot(ya_scr[rs, k2], pa_ref[k2, cs])
            out_b = acc_b_scr[rs, cs] + _dot(yb_scr[rs, k2], pb_ref[k2, cs])
            g_a = _sigmoid(gm_scr[rs, _cols(2 * n)] + bm_ref[0:1, cs])
            g_b = _sigmoid(gm_scr[rs, _cols(2 * n + 1)] + bm_ref[1:2, cs])
            mix_scr[rs, cs] = (g_a * out_a + g_b * out_b).astype(_bf16)
        return run

    def out_proj(r, n):
        def run():
            rs = slice(r * MH, (r + 1) * MH)
            cs = _cols(n)
            proj = _dot(mix_scr[rs, :], wo_ref[:, cs])
            for i in range(MH // TT):
                b = r * (MH // TT) + i
                res = x_ref[b, :, cs] + proj[_rows(i)]
                o_ref[b, :, cs] = res
                ssq[b] = ssq[b] + jnp.sum(res * res, axis=-1, keepdims=True)
        return run

    def out_norm(b):
        def run():
            o_ref[b] = o_ref[b] * lax.rsqrt(ssq[b] * (1.0 / D_MODEL) + EPS) * fg
        return run

    halves = range(M // MH)
    bph = MH // TT
    for r in halves:
        stage = [(1, merge(r, n)) for n in range(NCB)]
        before = [out_proj(r - 1, n) for n in range(NCB)] if r > 0 else []
        _emit_interleaved(stage, before)
        if r > 1:
            for b in range((r - 2) * bph, (r - 1) * bph):
                out_norm(b)()
    last = halves[-1]
    _emit_interleaved([(1, out_proj(last, n)) for n in range(NCB)],
                      [out_norm(b) for b in range((last - 1) * bph, last * bph)])
    for b in range(last * bph, NB):
        out_norm(b)()


def _resident(shape):
    zeros = (0,) * len(shape)
    return pl.BlockSpec(shape, lambda i, j: zeros, pipeline_mode=pl.Buffered(1))


@jax.jit
def kernel(x, w_in, b_merge, conv_w, conv_b, rg_wx, rg_bx, rg_wa, rg_ba, rg_lambda,
           hg_lb_logits, hg_norm_g, proj_a, proj_b, w_out, norm_g, final_norm_g):
    bsz, seq, d = x.shape
    assert d == D_MODEL and bsz % NB == 0 and seq % TT == 0
    assert w_in.shape[0] == 1, "single-layer block"

    w_in_b = w_in[0].astype(_bf16)
    lpb = CB // LRU_BLOCK_DIM
    wbd = jnp.zeros((NCB, CB, 2 * CB), _bf16)
    for c in range(NCB):
        for k in range(lpb):
            r = slice(k * LRU_BLOCK_DIM, (k + 1) * LRU_BLOCK_DIM)
            wbd = wbd.at[c, r, r].set(rg_wx[0, c * lpb + k].astype(_bf16))
            wbd = wbd.at[c, r, CB + k * LRU_BLOCK_DIM:CB + (k + 1) * LRU_BLOCK_DIM].set(
                rg_wa[0, c * lpb + k].astype(_bf16))
    pa = proj_a[0].astype(_bf16)
    pb = proj_b[0].astype(_bf16)
    wo = w_out[0].astype(_bf16)

    zeros = jnp.zeros((D_MODEL,), _f32)
    cvec = jnp.stack([conv_b[0], rg_bx[0].reshape(-1), rg_ba[0].reshape(-1), rg_lambda[0],
                      jnp.tile(hg_norm_g[0], D_MODEL // HEAD_DIM), zeros, zeros, zeros])
    cw = conv_w[0]
    lbl = hg_lb_logits
    bm = b_merge[0].reshape(2, D_MODEL)
    ng = norm_g[0].reshape(1, D_MODEL)
    fg = final_norm_g.reshape(1, D_MODEL)

    x_spec = pl.BlockSpec((NB, TT, D_MODEL), lambda i, j: (i, j, 0))
    operands = (w_in_b, wbd, pa, pb, wo, cw, cvec, lbl, bm, ng, fg)
    pitched = pltpu.VMEM((LT, NB * PITCH, LANES), _f32)
    return pl.pallas_call(
        _block_kernel,
        grid=(bsz // NB, seq // TT),
        in_specs=[x_spec] + [_resident(op.shape) for op in operands],
        out_specs=x_spec,
        out_shape=jax.ShapeDtypeStruct(x.shape, x.dtype),
        scratch_shapes=[
            pltpu.VMEM((M, D_MODEL), _bf16),
            pltpu.VMEM((M, NZ_SLOTS * CB), _f32),
            pltpu.VMEM((M, NZ_SLOTS * CB), _f32),
            pltpu.VMEM((M, 2 * D_MODEL), _f32),
            pltpu.VMEM((M, 2 * CB), _f32),
            pltpu.VMEM((NB, PITCH, CB), _f32),
            pltpu.VMEM((NCB, NB, 8, CB), _f32),
            pitched, pitched, pitched, pitched, pitched,
            pltpu.VMEM((NCB, NB, CB), _f32),
            pltpu.VMEM((NCB, NB, HPB, HEAD_DIM, HEAD_DIM), _f32),
            pltpu.VMEM((M, D_MODEL), _bf16),
            pltpu.VMEM((M, D_MODEL), _bf16),
            pltpu.VMEM((M, D_MODEL), _bf16),
            pltpu.VMEM((M, D_MODEL), _f32),
            pltpu.VMEM((M, D_MODEL), _f32),
        ],
        compiler_params=pltpu.CompilerParams(
            dimension_semantics=("arbitrary", "arbitrary"),
            vmem_limit_bytes=VMEM_LIMIT_BYTES),
        name="hawk_hgrn2_block",
    )(x, *operands)
```

```python
import jax
import jax.numpy as jnp
from jax import lax
from jax.experimental import pallas as pl
from jax.experimental.pallas import tpu as pltpu

D_MODEL = 1024
LRU_BLOCK_DIM = 128
CONV_WIDTH = 4
LRU_C = 8.0
HEAD_DIM = 128
CHUNK = 64
HG_SCALE = HEAD_DIM ** -0.5
EPS = 1e-6

NB = 8
TT = CHUNK
M = NB * TT
MH = M // 2
KH = D_MODEL // 2
CB = 256
NCB = D_MODEL // CB
HPB = CB // HEAD_DIM
LANES = 128
LT = CB // LANES
PITCH = TT + 8
SCAN_CHUNK = 8
VMEM_LIMIT_BYTES = 58 * 1024 * 1024

G_XA, G_GA, G_Q, G_F, G_I, G_GB, G_MA, G_MB = range(8)
Z_GROUPS = (G_XA, G_GA, G_Q, G_F, G_I, G_GB)
Z_XA, Z_GA, Z_Q, Z_F, Z_I, Z_GB = range(6)
NZ = len(Z_GROUPS)

_f32 = jnp.float32
_bf16 = jnp.bfloat16


def _dot(a, b):
    return jnp.dot(a, b, preferred_element_type=_f32)


def _dot_nt(a, b):
    return lax.dot_general(a, b, (((1,), (1,)), ((), ())), preferred_element_type=_f32)


def _dot_tn(a, b):
    return lax.dot_general(a, b, (((0,), (0,)), ((), ())), preferred_element_type=_f32)


def _sigmoid(x):
    return 0.5 * jnp.tanh(0.5 * x) + 0.5


def _silu(x):
    h = 0.5 * x
    return h * jnp.tanh(h) + h


def _sqrt_nonneg(x):
    return jnp.where(x > 0.0, x * lax.rsqrt(x), 0.0)


def _rows(b):
    return slice(b * TT, (b + 1) * TT)


def _cols(n):
    return slice(n * CB, (n + 1) * CB)


def _to_pitched(scr, b, val):
    for l in range(LT):
        scr[l, b * PITCH:b * PITCH + TT, :] = val[:, l * LANES:(l + 1) * LANES]


def _from_pitched(scr, b):
    return jnp.concatenate([scr[l, b * PITCH:b * PITCH + TT, :] for l in range(LT)], axis=1)


def _emit_interleaved(primary, secondary):
    total = sum(w for w, _ in primary)
    done, spent = 0, 0
    for w, piece in primary:
        piece()
        spent += w
        target = spent * len(secondary) // total
        while done < target:
            secondary[done]()
            done += 1
    for piece in secondary[done:]:
        piece()


def _block_kernel(x_ref, w_in_ref, wbd_ref, pa_ref, pb_ref, wo_ref,
                  cw_ref, cvec_ref, lbl_ref, bm_ref, ng_ref, fg_ref,
                  o_ref,
                  hb_scr, z_scr, gm_scr, gt_scr, xc_scr, tail_scr,
                  a_scr, u_scr, h_scr, lf_scr, bc_scr,
                  hcar_scr, st_scr, ya_scr, yb_scr, mix_scr, acc_a_scr, acc_b_scr):
    j = pl.program_id(1)

    @pl.when(j == 0)
    def _reset_carries():
        tail_scr[...] = jnp.zeros_like(tail_scr)
        hcar_scr[...] = jnp.zeros_like(hcar_scr)
        st_scr[...] = jnp.zeros_like(st_scr)

    ng = ng_ref[...]
    rinv_in = [None] * NB

    def in_scale(b):
        def run():
            xb = x_ref[b]
            rinv_in[b] = lax.rsqrt(jnp.mean(xb * xb, axis=-1, keepdims=True) + EPS)
        return run

    def in_norm(b):
        def run():
            hb_scr[_rows(b), :] = ((x_ref[b] * rinv_in[b]) * ng).astype(_bf16)
        return run

    causal = (lax.broadcasted_iota(jnp.int32, (TT, TT), 0)
              >= lax.broadcasted_iota(jnp.int32, (TT, TT), 1))
    z_holds = {}
    z_reads = set()

    def w_in_cols(group, c):
        start = group * D_MODEL + c * CB
        return w_in_ref[:, start:start + CB]

    def project_pieces(c, slots=range(NZ)):
        def piece(slot, r):
            def run():
                bph = MH // TT
                assert c == 0 or all((c - 1, slot, b) in z_reads for b in range(r * bph, (r + 1) * bph)), \
                    "projection piece emitted before the previous block's readers"
                rs = slice(r * MH, (r + 1) * MH)
                z_scr[rs, _cols(slot)] = _dot(hb_scr[rs, :], w_in_cols(Z_GROUPS[slot], c))
                z_holds[slot, r] = c
            return run
        return [piece(slot, r) for slot in slots for r in range(M // MH)]

    def merge_gate_pieces(n):
        def piece(k, r):
            def run():
                rs = slice(r * MH, (r + 1) * MH)
                gm_scr[rs, _cols(2 * n + k)] = _dot(hb_scr[rs, :], w_in_cols((G_MA, G_MB)[k], n))
            return run
        return [piece(k, r) for k in range(2) for r in range(M // MH)]

    def channel_block_pieces(c):
        cs = _cols(c)

        def zcols(slot, b):
            assert z_holds.get((slot, b * TT // MH)) == c, "projection piece emitted after its reader"
            z_reads.add((c, slot, b))
            return z_scr[_rows(b), _cols(slot)]

        conv_b, bx, ba, lam, hg_g = (cvec_ref[k:k + 1, cs] for k in range(5))
        cw = cw_ref[:, cs]
        xa = [None] * NB
        kk = [None] * NB

        def conv(b):
            def run():
                xc_scr[b, 0:8, :] = tail_scr[c, b]
                xc_scr[b, 8:8 + TT, :] = zcols(Z_XA, b)
                acc = conv_b + cw[3:4] * xc_scr[b, 8:8 + TT, :]
                for k in range(CONV_WIDTH - 1):
                    acc = acc + cw[k:k + 1] * xc_scr[b, 5 + k:5 + k + TT, :]
                xa[b] = acc
                tail_scr[c, b] = xc_scr[b, TT:TT + 8, :]
            return run

        def gate_dot():
            gt_scr[...] = _dot(jnp.concatenate(xa, axis=0).astype(_bf16), wbd_ref[c])

        def gates(b):
            def run():
                neg_lam = -lam
                softplus = jnp.maximum(neg_lam, 0.0) + jnp.log(1.0 + jnp.exp(-jnp.abs(neg_lam)))
                gate_i = _sigmoid(gt_scr[_rows(b), :CB] + bx)
                gate_r = _sigmoid(gt_scr[_rows(b), CB:] + ba)
                a = jnp.exp(((-LRU_C) * softplus) * gate_r)
                mult = _sqrt_nonneg(1.0 - a * a)
                _to_pitched(a_scr, b, a)
                _to_pitched(u_scr, b, mult * gate_i * xa[b])
            return run

        def forget(b):
            def run():
                lbl = lbl_ref[:, cs]
                lmax = jnp.maximum(lbl[0:1], lbl[1:2])
                e0 = jnp.exp(lbl[0:1] - lmax)
                e1 = jnp.exp(lbl[1:2] - lmax)
                lb = e0 / (e0 + e1)
                f = lb + (1.0 - lb) * _sigmoid(zcols(Z_F, b))
                _to_pitched(lf_scr, b, jnp.log(f))
                kk[b] = 1.0 - f
            return run

        cums = [jnp.zeros((NB, LANES), _f32) for _ in range(LT)]
        hs = [None] * LT

        def cumsum_steps(t0):
            def run():
                for t in range(t0, t0 + SCAN_CHUNK):
                    rows = pl.ds(t, NB, stride=PITCH)
                    for l in range(LT):
                        cums[l] = cums[l] + lf_scr[l, rows, :]
                        bc_scr[l, rows, :] = cums[l]
            return run

        def lru_steps(t0):
            def run():
                if t0 == 0:
                    h0 = hcar_scr[c]
                    for l in range(LT):
                        hs[l] = h0[:, l * LANES:(l + 1) * LANES]
                for t in range(t0, t0 + SCAN_CHUNK):
                    rows = pl.ds(t, NB, stride=PITCH)
                    for l in range(LT):
                        hs[l] = a_scr[l, rows, :] * hs[l] + u_scr[l, rows, :]
                        h_scr[l, rows, :] = hs[l]
                if t0 + SCAN_CHUNK == TT:
                    hcar_scr[c] = jnp.concatenate(hs, axis=1)
            return run

        def gated_a(b):
            def run():
                ya_scr[_rows(b), cs] = (_from_pitched(h_scr, b) * _silu(zcols(Z_GA, b))).astype(_bf16)
            return run

        att_in = [None] * NB
        att_w = {}
        o_raw = {}
        heads = [(b, hh) for b in range(NB) for hh in range(HPB)]

        def head_cols(hh):
            return slice(hh * HEAD_DIM, (hh + 1) * HEAD_DIM)

        def att_prologue(b):
            def run():
                bc = _from_pitched(bc_scr, b)
                b_mid = bc[CHUNK // 2:CHUNK // 2 + 1]
                b_last = bc[CHUNK - 1:CHUNK]
                q_in = _silu(zcols(Z_Q, b)) * jnp.exp(bc - b_mid) * HG_SCALE
                k_in = kk[b] * jnp.exp(b_mid - bc)
                q_inter = (q_in * jnp.exp(b_mid)).astype(_bf16)
                k_state = (k_in * jnp.exp(b_last - b_mid)).astype(_bf16)
                att_in[b] = (q_in.astype(_bf16), k_in.astype(_bf16), q_inter, k_state,
                             zcols(Z_I, b).astype(_bf16), jnp.exp(b_last))
            return run

        def att_scores(b, hh):
            def run():
                q_in, k_in = att_in[b][0], att_in[b][1]
                ls = head_cols(hh)
                att_w[b, hh] = jnp.where(causal, _dot_nt(q_in[:, ls], k_in[:, ls]), 0.0).astype(_bf16)
            return run

        def att_output(b, hh):
            def run():
                q_inter, v = att_in[b][2], att_in[b][4]
                ls = head_cols(hh)
                st = st_scr[c, b, hh]
                o_raw[b, hh] = _dot(att_w[b, hh], v[:, ls]) + _dot_nt(q_inter[:, ls], st.astype(_bf16))
            return run

        def state_update(b, hh):
            def run():
                k_state, v, decay = att_in[b][3], att_in[b][4], att_in[b][5]
                ls = head_cols(hh)
                st_scr[c, b, hh] = st_scr[c, b, hh] * decay[:, ls] + _dot_tn(v[:, ls], k_state[:, ls])
            return run

        def gated_b(b):
            def run():
                o_heads = []
                for hh in range(HPB):
                    o_h = o_raw[b, hh]
                    ms = jnp.mean(o_h * o_h, axis=-1, keepdims=True)
                    o_heads.append(o_h * lax.rsqrt(ms + EPS))
                o_n = jnp.concatenate(o_heads, axis=1) * hg_g
                yb_scr[_rows(b), cs] = (o_n * _silu(zcols(Z_GB, b))).astype(_bf16)
            return run

        def next_block(*slots):
            return [(0, p) for p in project_pieces(c + 1, slots)] if c + 1 < NCB else []

        steps = range(0, TT, SCAN_CHUNK)
        pieces = [(12, conv(b)) for b in range(NB)] + next_block(Z_XA) + [(2, gate_dot)]
        pieces += [(9, forget(b)) for b in range(NB)] + next_block(Z_F)
        for b, t0 in zip(range(NB), steps):
            pieces += [(22, gates(b)), (3, cumsum_steps(t0))]
        for b, t0 in zip(range(NB), steps):
            pieces += [(20, att_prologue(b)), (4, lru_steps(t0))]
        pieces += next_block(Z_Q, Z_I)
        pieces += [(1, att_scores(b, hh)) for b, hh in heads]
        pieces += [(7, gated_a(b)) for b in range(NB // 2)]
        pieces += [(2, att_output(b, hh)) for b, hh in heads]
        pieces += [(7, gated_a(b)) for b in range(NB // 2, NB)] + next_block(Z_GA)
        pieces += [(3, state_update(b, hh)) for b, hh in heads]
        pieces += [(14, gated_b(b)) for b in range(NB)] + next_block(Z_GB)
        return pieces

    def branch_partial_pieces():
        ks = slice(0, KH)

        def piece(src, w_ref, acc, n, r):
            def run():
                rs = slice(r * MH, (r + 1) * MH)
                acc[rs, _cols(n)] = _dot(src[rs, ks], w_ref[ks, _cols(n)])
            return run
        return [piece(src, w_ref, acc, n, r)
                for n in range(NCB) for r in range(M // MH)
                for src, w_ref, acc in ((ya_scr, pa_ref, acc_a_scr), (yb_scr, pb_ref, acc_b_scr))]

    head_z = project_pieces(0)
    first = [p for i, p in enumerate(head_z) if i // (M // MH) in (Z_XA, Z_F)]
    rest = [p for i, p in enumerate(head_z) if i // (M // MH) not in (Z_XA, Z_F)]
    for b in range(NB):
        in_scale(b)()
    for b in range(NB // 2):
        in_norm(b)()
    _emit_interleaved([(1, in_norm(b)) for b in range(NB // 2, NB)], first[0::2])
    for piece in first[1::2]:
        piece()

    for c in range(NCB):
        filler = rest if c == 0 else []
        filler = filler + merge_gate_pieces(c)
        if c == NCB - 1:
            filler = filler + branch_partial_pieces()
        _emit_interleaved(channel_block_pieces(c), filler)

    ssq = [jnp.zeros((TT, 1), _f32) for _ in range(NB)]
    fg = fg_ref[...]
    k2 = slice(KH, D_MODEL)

    def merge(r, n):
        def run():
            rs = slice(r * MH, (r + 1) * MH)
            cs = _cols(n)
            out_a = acc_a_scr[rs, cs] + _dot(ya_scr[rs, k2], pa_ref[k2, cs])
            out_b = acc_b_scr[rs, cs] + _dot(yb_scr[rs, k2], pb_ref[k2, cs])
            g_a = _sigmoid(gm_scr[rs, _cols(2 * n)] + bm_ref[0:1, cs])
            g_b = _sigmoid(gm_scr[rs, _cols(2 * n + 1)] + bm_ref[1:2, cs])
            mix_scr[rs, cs] = (g_a * out_a + g_b * out_b).astype(_bf16)
        return run

    def out_proj(r, n):
        def run():
            rs = slice(r * MH, (r + 1) * MH)
            cs = _cols(n)
            proj = _dot(mix_scr[rs, :], wo_ref[:, cs])
            for i in range(MH // TT):
                b = r * (MH // TT) + i
                res = x_ref[b, :, cs] + proj[_rows(i)]
                o_ref[b, :, cs] = res
                ssq[b] = ssq[b] + jnp.sum(res * res, axis=-1, keepdims=True)
        return run

    def out_norm(b):
        def run():
            o_ref[b] = o_ref[b] * lax.rsqrt(ssq[b] * (1.0 / D_MODEL) + EPS) * fg
        return run

    halves = range(M // MH)
    bph = MH // TT
    for r in halves:
        stage = [(1, merge(r, n)) for n in range(NCB)]
        before = [out_proj(r - 1, n) for n in range(NCB)] if r > 0 else []
        _emit_interleaved(stage, before)
        if r > 1:
            for b in range((r - 2) * bph, (r - 1) * bph):
                out_norm(b)()
    last = halves[-1]
    _emit_interleaved([(1, out_proj(last, n)) for n in range(NCB)],
                      [out_norm(b) for b in range((last - 1) * bph, last * bph)])
    for b in range(last * bph, NB):
        out_norm(b)()


def _resident(shape):
    zeros = (0,) * len(shape)
    return pl.BlockSpec(shape, lambda i, j: zeros, pipeline_mode=pl.Buffered(1))


@jax.jit
def kernel(x, w_in, b_merge, conv_w, conv_b, rg_wx, rg_bx, rg_wa, rg_ba, rg_lambda,
           hg_lb_logits, hg_norm_g, proj_a, proj_b, w_out, norm_g, final_norm_g):
    bsz, seq, d = x.shape
    assert d == D_MODEL and bsz % NB == 0 and seq % TT == 0
    assert w_in.shape[0] == 1, "single-layer block"

    w_in_b = w_in[0].astype(_bf16)
    lpb = CB // LRU_BLOCK_DIM
    wbd = jnp.zeros((NCB, CB, 2 * CB), _bf16)
    for c in range(NCB):
        for k in range(lpb):
            r = slice(k * LRU_BLOCK_DIM, (k + 1) * LRU_BLOCK_DIM)
            wbd = wbd.at[c, r, r].set(rg_wx[0, c * lpb + k].astype(_bf16))
            wbd = wbd.at[c, r, CB + k * LRU_BLOCK_DIM:CB + (k + 1) * LRU_BLOCK_DIM].set(
                rg_wa[0, c * lpb + k].astype(_bf16))
    pa = proj_a[0].astype(_bf16)
    pb = proj_b[0].astype(_bf16)
    wo = w_out[0].astype(_bf16)

    zeros = jnp.zeros((D_MODEL,), _f32)
    cvec = jnp.stack([conv_b[0], rg_bx[0].reshape(-1), rg_ba[0].reshape(-1), rg_lambda[0],
                      jnp.tile(hg_norm_g[0], D_MODEL // HEAD_DIM), zeros, zeros, zeros])
    cw = conv_w[0]
    lbl = hg_lb_logits
    bm = b_merge[0].reshape(2, D_MODEL)
    ng = norm_g[0].reshape(1, D_MODEL)
    fg = final_norm_g.reshape(1, D_MODEL)

    x_spec = pl.BlockSpec((NB, TT, D_MODEL), lambda i, j: (i, j, 0))
    operands = (w_in_b, wbd, pa, pb, wo, cw, cvec, lbl, bm, ng, fg)
    pitched = pltpu.VMEM((LT, NB * PITCH, LANES), _f32)
    return pl.pallas_call(
        _block_kernel,
        grid=(bsz // NB, seq // TT),
        in_specs=[x_spec] + [_resident(op.shape) for op in operands],
        out_specs=x_spec,
        out_shape=jax.ShapeDtypeStruct(x.shape, x.dtype),
        scratch_shapes=[
            pltpu.VMEM((M, D_MODEL), _bf16),
            pltpu.VMEM((M, NZ * CB), _f32),
            pltpu.VMEM((M, 2 * D_MODEL), _f32),
            pltpu.VMEM((M, 2 * CB), _f32),
            pltpu.VMEM((NB, PITCH, CB), _f32),
            pltpu.VMEM((NCB, NB, 8, CB), _f32),
            pitched, pitched, pitched, pitched, pitched,
            pltpu.VMEM((NCB, NB, CB), _f32),
            pltpu.VMEM((NCB, NB, HPB, HEAD_DIM, HEAD_DIM), _f32),
            pltpu.VMEM((M, D_MODEL), _bf16),
            pltpu.VMEM((M, D_MODEL), _bf16),
            pltpu.VMEM((M, D_MODEL), _bf16),
            pltpu.VMEM((M, D_MODEL), _f32),
            pltpu.VMEM((M, D_MODEL), _f32),
        ],
        compiler_params=pltpu.CompilerParams(
            dimension_semantics=("arbitrary", "arbitrary"),
            vmem_limit_bytes=VMEM_LIMIT_BYTES),
        name="hawk_hgrn2_block",
    )(x, *operands)
```

```python
import jax
import jax.numpy as jnp
from jax import lax
from jax.experimental import pallas as pl
from jax.experimental.pallas import tpu as pltpu

D_MODEL = 1024
LRU_BLOCK_DIM = 128
CONV_WIDTH = 4
LRU_C = 8.0
HEAD_DIM = 128
CHUNK = 64
HG_SCALE = HEAD_DIM ** -0.5
EPS = 1e-6

NB = 8
TT = CHUNK
M = NB * TT
MH = M
CB = 256
NCB = D_MODEL // CB
HPB = CB // HEAD_DIM
LANES = 128
LT = CB // LANES
PITCH = TT + 8
SCAN_CHUNK = 8
VMEM_LIMIT_BYTES = 58 * 1024 * 1024

G_XA, G_GA, G_Q, G_F, G_I, G_GB, G_MA, G_MB = range(8)
Z_GROUPS = (G_XA, G_GA, G_Q, G_F, G_I, G_GB)
Z_XA, Z_GA, Z_Q, Z_F, Z_I, Z_GB = range(6)
NZ = len(Z_GROUPS)

_f32 = jnp.float32
_bf16 = jnp.bfloat16


def _dot(a, b):
    return jnp.dot(a, b, preferred_element_type=_f32)


def _dot_nt(a, b):
    return lax.dot_general(a, b, (((1,), (1,)), ((), ())), preferred_element_type=_f32)


def _dot_tn(a, b):
    return lax.dot_general(a, b, (((0,), (0,)), ((), ())), preferred_element_type=_f32)


def _sigmoid(x):
    return 0.5 * jnp.tanh(0.5 * x) + 0.5


def _silu(x):
    h = 0.5 * x
    return h * jnp.tanh(h) + h


def _sqrt_nonneg(x):
    return jnp.where(x > 0.0, x * lax.rsqrt(x), 0.0)


def _rows(b):
    return slice(b * TT, (b + 1) * TT)


def _cols(n):
    return slice(n * CB, (n + 1) * CB)


def _to_pitched(scr, b, val):
    for l in range(LT):
        scr[l, b * PITCH:b * PITCH + TT, :] = val[:, l * LANES:(l + 1) * LANES]


def _from_pitched(scr, b):
    return jnp.concatenate([scr[l, b * PITCH:b * PITCH + TT, :] for l in range(LT)], axis=1)


def _emit_interleaved(primary, secondary):
    done = 0
    for i, piece in enumerate(primary):
        piece()
        target = (i + 1) * len(secondary) // len(primary)
        while done < target:
            secondary[done]()
            done += 1
    for piece in secondary[done:]:
        piece()


def _block_kernel(x_ref, w_in_ref, wbd_ref, pa_ref, pb_ref, wo_ref,
                  cw_ref, cvec_ref, lbl_ref, bm_ref, ng_ref, fg_ref,
                  o_ref,
                  hb_scr, z0_scr, z1_scr, gm_scr, gt_scr, xc_scr, tail_scr,
                  a_scr, u_scr, h_scr, lf_scr, bc_scr,
                  hcar_scr, st_scr, ya_scr, yb_scr, mix_scr):
    j = pl.program_id(1)

    @pl.when(j == 0)
    def _reset_carries():
        tail_scr[...] = jnp.zeros_like(tail_scr)
        hcar_scr[...] = jnp.zeros_like(hcar_scr)
        st_scr[...] = jnp.zeros_like(st_scr)

    ng = ng_ref[...]
    for b in range(NB):
        xb = x_ref[b]
        ms = jnp.mean(xb * xb, axis=-1, keepdims=True)
        hb_scr[_rows(b), :] = ((xb * lax.rsqrt(ms + EPS)) * ng).astype(_bf16)

    causal = (lax.broadcasted_iota(jnp.int32, (TT, TT), 0)
              >= lax.broadcasted_iota(jnp.int32, (TT, TT), 1))
    z_bufs = (z0_scr, z1_scr)

    def w_in_cols(group, c):
        start = group * D_MODEL + c * CB
        return w_in_ref[:, start:start + CB]

    def project_pieces(c):
        z_scr = z_bufs[c % 2]

        def piece(slot, r):
            def run():
                rs = slice(r * MH, (r + 1) * MH)
                z_scr[rs, _cols(slot)] = _dot(hb_scr[rs, :], w_in_cols(Z_GROUPS[slot], c))
            return run
        return [piece(slot, r) for slot in range(NZ) for r in range(M // MH)]

    def merge_gate_pieces(n):
        def piece(k, r):
            def run():
                rs = slice(r * MH, (r + 1) * MH)
                gm_scr[rs, _cols(2 * n + k)] = _dot(hb_scr[rs, :], w_in_cols((G_MA, G_MB)[k], n))
            return run
        return [piece(k, r) for k in range(2) for r in range(M // MH)]

    def channel_block_pieces(c):
        z_scr = z_bufs[c % 2]
        cs = _cols(c)

        def zcols(slot, b):
            return z_scr[_rows(b), _cols(slot)]

        conv_b, bx, ba, lam, hg_g = (cvec_ref[k:k + 1, cs] for k in range(5))
        cw = cw_ref[:, cs]
        xa = [None] * NB
        kk = [None] * NB

        def conv(b):
            def run():
                xc_scr[b, 0:8, :] = tail_scr[c, b]
                xc_scr[b, 8:8 + TT, :] = zcols(Z_XA, b)
                acc = conv_b + cw[3:4] * xc_scr[b, 8:8 + TT, :]
                for k in range(CONV_WIDTH - 1):
                    acc = acc + cw[k:k + 1] * xc_scr[b, 5 + k:5 + k + TT, :]
                xa[b] = acc
                tail_scr[c, b] = xc_scr[b, TT:TT + 8, :]
            return run

        def gate_dot():
            gt_scr[...] = _dot(jnp.concatenate(xa, axis=0).astype(_bf16), wbd_ref[c])

        def gates(b):
            def run():
                neg_lam = -lam
                softplus = jnp.maximum(neg_lam, 0.0) + jnp.log(1.0 + jnp.exp(-jnp.abs(neg_lam)))
                gate_i = _sigmoid(gt_scr[_rows(b), :CB] + bx)
                gate_r = _sigmoid(gt_scr[_rows(b), CB:] + ba)
                a = jnp.exp(((-LRU_C) * softplus) * gate_r)
                mult = _sqrt_nonneg(1.0 - a * a)
                _to_pitched(a_scr, b, a)
                _to_pitched(u_scr, b, mult * gate_i * xa[b])
            return run

        def forget(b):
            def run():
                lbl = lbl_ref[:, cs]
                lmax = jnp.maximum(lbl[0:1], lbl[1:2])
                e0 = jnp.exp(lbl[0:1] - lmax)
                e1 = jnp.exp(lbl[1:2] - lmax)
                lb = e0 / (e0 + e1)
                f = lb + (1.0 - lb) * _sigmoid(zcols(Z_F, b))
                _to_pitched(lf_scr, b, jnp.log(f))
                kk[b] = 1.0 - f
            return run

        cums = [jnp.zeros((NB, LANES), _f32) for _ in range(LT)]
        hs = [None] * LT

        def cumsum_steps(t0):
            def run():
                for t in range(t0, t0 + SCAN_CHUNK):
                    rows = pl.ds(t, NB, stride=PITCH)
                    for l in range(LT):
                        cums[l] = cums[l] + lf_scr[l, rows, :]
                        bc_scr[l, rows, :] = cums[l]
            return run

        def lru_steps(t0):
            def run():
                if t0 == 0:
                    h0 = hcar_scr[c]
                    for l in range(LT):
                        hs[l] = h0[:, l * LANES:(l + 1) * LANES]
                for t in range(t0, t0 + SCAN_CHUNK):
                    rows = pl.ds(t, NB, stride=PITCH)
                    for l in range(LT):
                        hs[l] = a_scr[l, rows, :] * hs[l] + u_scr[l, rows, :]
                        h_scr[l, rows, :] = hs[l]
                if t0 + SCAN_CHUNK == TT:
                    hcar_scr[c] = jnp.concatenate(hs, axis=1)
            return run

        def gated_a(b):
            def run():
                ya_scr[_rows(b), cs] = (_from_pitched(h_scr, b) * _silu(zcols(Z_GA, b))).astype(_bf16)
            return run

        att_in = [None] * NB
        att_w = {}
        o_raw = {}
        heads = [(b, hh) for b in range(NB) for hh in range(HPB)]

        def head_cols(hh):
            return slice(hh * HEAD_DIM, (hh + 1) * HEAD_DIM)

        def att_prologue(b):
            def run():
                bc = _from_pitched(bc_scr, b)
                b_mid = bc[CHUNK // 2:CHUNK // 2 + 1]
                b_last = bc[CHUNK - 1:CHUNK]
                q_in = _silu(zcols(Z_Q, b)) * jnp.exp(bc - b_mid) * HG_SCALE
                k_in = kk[b] * jnp.exp(b_mid - bc)
                q_inter = (q_in * jnp.exp(b_mid)).astype(_bf16)
                k_state = (k_in * jnp.exp(b_last - b_mid)).astype(_bf16)
                att_in[b] = (q_in.astype(_bf16), k_in.astype(_bf16), q_inter, k_state,
                             zcols(Z_I, b).astype(_bf16), jnp.exp(b_last))
            return run

        def att_scores(b, hh):
            def run():
                q_in, k_in = att_in[b][0], att_in[b][1]
                ls = head_cols(hh)
                att_w[b, hh] = jnp.where(causal, _dot_nt(q_in[:, ls], k_in[:, ls]), 0.0).astype(_bf16)
            return run

        def att_output(b, hh):
            def run():
                q_inter, v = att_in[b][2], att_in[b][4]
                ls = head_cols(hh)
                st = st_scr[c, b, hh]
                o_raw[b, hh] = _dot(att_w[b, hh], v[:, ls]) + _dot_nt(q_inter[:, ls], st.astype(_bf16))
            return run

        def state_update(b, hh):
            def run():
                k_state, v, decay = att_in[b][3], att_in[b][4], att_in[b][5]
                ls = head_cols(hh)
                st_scr[c, b, hh] = st_scr[c, b, hh] * decay[:, ls] + _dot_tn(v[:, ls], k_state[:, ls])
            return run

        def gated_b(b):
            def run():
                o_heads = []
                for hh in range(HPB):
                    o_h = o_raw[b, hh]
                    ms = jnp.mean(o_h * o_h, axis=-1, keepdims=True)
                    o_heads.append(o_h * lax.rsqrt(ms + EPS))
                o_n = jnp.concatenate(o_heads, axis=1) * hg_g
                yb_scr[_rows(b), cs] = (o_n * _silu(zcols(Z_GB, b))).astype(_bf16)
            return run

        steps = range(0, TT, SCAN_CHUNK)
        pieces = [conv(b) for b in range(NB)] + [gate_dot] + [forget(b) for b in range(NB)]
        for b, t0 in zip(range(NB), steps):
            pieces += [gates(b), cumsum_steps(t0)]
        for b, t0 in zip(range(NB), steps):
            pieces += [att_prologue(b), lru_steps(t0)]
        pieces += [att_scores(b, hh) for b, hh in heads]
        pieces += [gated_a(b) for b in range(NB // 2)]
        pieces += [att_output(b, hh) for b, hh in heads]
        pieces += [gated_a(b) for b in range(NB // 2, NB)]
        pieces += [state_update(b, hh) for b, hh in heads]
        pieces += [gated_b(b) for b in range(NB)]
        return pieces

    for piece in project_pieces(0):
        piece()
    for c in range(NCB):
        filler = project_pieces(c + 1) if c + 1 < NCB else []
        if c >= NCB - 2:
            base = 2 * (c - (NCB - 2))
            filler = filler + merge_gate_pieces(base) + merge_gate_pieces(base + 1)
        _emit_interleaved(channel_block_pieces(c), filler)

    for n in range(NCB):
        cs = _cols(n)
        out_a = _dot(ya_scr[...], pa_ref[:, cs])
        out_b = _dot(yb_scr[...], pb_ref[:, cs])
        g_a = _sigmoid(gm_scr[:, _cols(2 * n)] + bm_ref[0:1, cs])
        g_b = _sigmoid(gm_scr[:, _cols(2 * n + 1)] + bm_ref[1:2, cs])
        mix_scr[:, cs] = (g_a * out_a + g_b * out_b).astype(_bf16)

    ssq = [jnp.zeros((TT, 1), _f32) for _ in range(NB)]
    for n in range(NCB):
        cs = _cols(n)
        proj = _dot(mix_scr[...], wo_ref[:, cs])
        for b in range(NB):
            res = x_ref[b, :, cs] + proj[_rows(b)]
            o_ref[b, :, cs] = res
            ssq[b] = ssq[b] + jnp.sum(res * res, axis=-1, keepdims=True)
    fg = fg_ref[...]
    for b in range(NB):
        o_ref[b] = o_ref[b] * lax.rsqrt(ssq[b] * (1.0 / D_MODEL) + EPS) * fg


def _resident(shape):
    zeros = (0,) * len(shape)
    return pl.BlockSpec(shape, lambda i, j: zeros, pipeline_mode=pl.Buffered(1))


@jax.jit
def kernel(x, w_in, b_merge, conv_w, conv_b, rg_wx, rg_bx, rg_wa, rg_ba, rg_lambda,
           hg_lb_logits, hg_norm_g, proj_a, proj_b, w_out, norm_g, final_norm_g):
    bsz, seq, d = x.shape
    assert d == D_MODEL and bsz % NB == 0 and seq % TT == 0
    assert w_in.shape[0] == 1, "single-layer block"

    w_in_b = w_in[0].astype(_bf16)
    lpb = CB // LRU_BLOCK_DIM
    wbd = jnp.zeros((NCB, CB, 2 * CB), _bf16)
    for c in range(NCB):
        for k in range(lpb):
            r = slice(k * LRU_BLOCK_DIM, (k + 1) * LRU_BLOCK_DIM)
            wbd = wbd.at[c, r, r].set(rg_wx[0, c * lpb + k].astype(_bf16))
            wbd = wbd.at[c, r, CB + k * LRU_BLOCK_DIM:CB + (k + 1) * LRU_BLOCK_DIM].set(
                rg_wa[0, c * lpb + k].astype(_bf16))
    pa = proj_a[0].astype(_bf16)
    pb = proj_b[0].astype(_bf16)
    wo = w_out[0].astype(_bf16)

    zeros = jnp.zeros((D_MODEL,), _f32)
    cvec = jnp.stack([conv_b[0], rg_bx[0].reshape(-1), rg_ba[0].reshape(-1), rg_lambda[0],
                      jnp.tile(hg_norm_g[0], D_MODEL // HEAD_DIM), zeros, zeros, zeros])
    cw = conv_w[0]
    lbl = hg_lb_logits
    bm = b_merge[0].reshape(2, D_MODEL)
    ng = norm_g[0].reshape(1, D_MODEL)
    fg = final_norm_g.reshape(1, D_MODEL)

    x_spec = pl.BlockSpec((NB, TT, D_MODEL), lambda i, j: (i, j, 0))
    operands = (w_in_b, wbd, pa, pb, wo, cw, cvec, lbl, bm, ng, fg)
    pitched = pltpu.VMEM((LT, NB * PITCH, LANES), _f32)
    return pl.pallas_call(
        _block_kernel,
        grid=(bsz // NB, seq // TT),
        in_specs=[x_spec] + [_resident(op.shape) for op in operands],
        out_specs=x_spec,
        out_shape=jax.ShapeDtypeStruct(x.shape, x.dtype),
        scratch_shapes=[
            pltpu.VMEM((M, D_MODEL), _bf16),
            pltpu.VMEM((M, NZ * CB), _f32),
            pltpu.VMEM((M, NZ * CB), _f32),
            pltpu.VMEM((M, 2 * D_MODEL), _f32),
            pltpu.VMEM((M, 2 * CB), _f32),
            pltpu.VMEM((NB, PITCH, CB), _f32),
            pltpu.VMEM((NCB, NB, 8, CB), _f32),
            pitched, pitched, pitched, pitched, pitched,
            pltpu.VMEM((NCB, NB, CB), _f32),
            pltpu.VMEM((NCB, NB, HPB, HEAD_DIM, HEAD_DIM), _f32),
            pltpu.VMEM((M, D_MODEL), _bf16),
            pltpu.VMEM((M, D_MODEL), _bf16),
            pltpu.VMEM((M, D_MODEL), _bf16),
        ],
        compiler_params=pltpu.CompilerParams(
            dimension_semantics=("arbitrary", "arbitrary"),
            vmem_limit_bytes=VMEM_LIMIT_BYTES),
        name="hawk_hgrn2_block",
    )(x, *operands)
```

```python
import jax
import jax.numpy as jnp
from jax import lax
from jax.experimental import pallas as pl
from jax.experimental.pallas import tpu as pltpu

D_MODEL = 1024
LRU_BLOCK_DIM = 128
CONV_WIDTH = 4
LRU_C = 8.0
HEAD_DIM = 128
CHUNK = 64
HG_SCALE = HEAD_DIM ** -0.5
EPS = 1e-6

NB = 8
TT = CHUNK
M = NB * TT
MH = M
CB = 256
NCB = D_MODEL // CB
HPB = CB // HEAD_DIM
LANES = 128
LT = CB // LANES
PITCH = TT + 8
SCAN_CHUNK = 8
WROWS = 32
VMEM_LIMIT_BYTES = 58 * 1024 * 1024

G_XA, G_GA, G_Q, G_F, G_I, G_GB, G_MA, G_MB = range(8)
Z_GROUPS = (G_XA, G_GA, G_Q, G_F, G_I, G_GB)
Z_XA, Z_GA, Z_Q, Z_F, Z_I, Z_GB = range(6)
NZ = len(Z_GROUPS)

_f32 = jnp.float32
_bf16 = jnp.bfloat16


def _dot(a, b):
    return jnp.dot(a, b, preferred_element_type=_f32)


def _dot_nt(a, b):
    return lax.dot_general(a, b, (((1,), (1,)), ((), ())), preferred_element_type=_f32)


def _dot_tn(a, b):
    return lax.dot_general(a, b, (((0,), (0,)), ((), ())), preferred_element_type=_f32)


def _sigmoid(x):
    return 0.5 * jnp.tanh(0.5 * x) + 0.5


def _silu(x):
    h = 0.5 * x
    return h * jnp.tanh(h) + h


def _sqrt_nonneg(x):
    return jnp.where(x > 0.0, x * lax.rsqrt(x), 0.0)


def _rows(b):
    return slice(b * TT, (b + 1) * TT)


def _cols(n):
    return slice(n * CB, (n + 1) * CB)


def _to_pitched(scr, b, val):
    for l in range(LT):
        scr[l, b * PITCH:b * PITCH + TT, :] = val[:, l * LANES:(l + 1) * LANES]


def _from_pitched(scr, b):
    return jnp.concatenate([scr[l, b * PITCH:b * PITCH + TT, :] for l in range(LT)], axis=1)


def _emit_interleaved(primary, secondary):
    done = 0
    for i, piece in enumerate(primary):
        piece()
        target = (i + 1) * len(secondary) // len(primary)
        while done < target:
            secondary[done]()
            done += 1
    for piece in secondary[done:]:
        piece()


def _block_kernel(x_ref, w_in_hbm, pa_hbm, pb_hbm, wo_hbm, wbd_ref,
                  cw_ref, cvec_ref, lbl_ref, bm_ref, ng_ref, fg_ref,
                  o_ref,
                  w_in_ref, pa_ref, pb_ref, wo_ref, wstage, pstage, wsem,
                  hb_scr, z0_scr, z1_scr, gm_scr, gt_scr, xc_scr, tail_scr,
                  a_scr, u_scr, h_scr, lf_scr, bc_scr,
                  hcar_scr, st_scr, ya_scr, yb_scr, mix_scr):
    i = pl.program_id(0)
    j = pl.program_id(1)

    def weight_rows(k):
        return pl.ds(pl.multiple_of(k * WROWS, WROWS), WROWS)

    def weight_copies(k, slot):
        rows = weight_rows(k)
        small = ((pa_hbm, 0), (pb_hbm, 1), (wo_hbm, 2))
        return [pltpu.make_async_copy(w_in_hbm.at[0, rows, :], wstage.at[slot], wsem.at[slot, 0])] + [
            pltpu.make_async_copy(src.at[0, rows, :], pstage.at[slot, n], wsem.at[slot, n + 1])
            for src, n in small]

    @pl.when((i == 0) & (j == 0))
    def _load_weights():
        n_chunks = D_MODEL // WROWS
        for cp in weight_copies(0, 0):
            cp.start()

        def chunk(k, carry):
            slot = k % 2

            @pl.when(k + 1 < n_chunks)
            def _prefetch():
                for cp in weight_copies(k + 1, 1 - slot):
                    cp.start()

            for cp in weight_copies(k, slot):
                cp.wait()
            rows = weight_rows(k)
            w_in_ref[rows, :] = wstage[slot].astype(_bf16)
            for n, dst in enumerate((pa_ref, pb_ref, wo_ref)):
                dst[rows, :] = pstage[slot, n].astype(_bf16)
            return carry

        lax.fori_loop(0, n_chunks, chunk, 0)

    @pl.when(j == 0)
    def _reset_carries():
        tail_scr[...] = jnp.zeros_like(tail_scr)
        hcar_scr[...] = jnp.zeros_like(hcar_scr)
        st_scr[...] = jnp.zeros_like(st_scr)

    ng = ng_ref[...]
    for b in range(NB):
        xb = x_ref[b]
        ms = jnp.mean(xb * xb, axis=-1, keepdims=True)
        hb_scr[_rows(b), :] = ((xb * lax.rsqrt(ms + EPS)) * ng).astype(_bf16)

    causal = (lax.broadcasted_iota(jnp.int32, (TT, TT), 0)
              >= lax.broadcasted_iota(jnp.int32, (TT, TT), 1))
    z_bufs = (z0_scr, z1_scr)

    def w_in_cols(group, c):
        start = group * D_MODEL + c * CB
        return w_in_ref[:, start:start + CB]

    def project_pieces(c):
        z_scr = z_bufs[c % 2]

        def piece(slot, r):
            def run():
                rs = slice(r * MH, (r + 1) * MH)
                z_scr[rs, _cols(slot)] = _dot(hb_scr[rs, :], w_in_cols(Z_GROUPS[slot], c))
            return run
        return [piece(slot, r) for slot in range(NZ) for r in range(M // MH)]

    def merge_gate_pieces(n):
        def piece(k, r):
            def run():
                rs = slice(r * MH, (r + 1) * MH)
                gm_scr[rs, _cols(2 * n + k)] = _dot(hb_scr[rs, :], w_in_cols((G_MA, G_MB)[k], n))
            return run
        return [piece(k, r) for k in range(2) for r in range(M // MH)]

    def channel_block_pieces(c):
        z_scr = z_bufs[c % 2]
        cs = _cols(c)

        def zcols(slot, b):
            return z_scr[_rows(b), _cols(slot)]

        conv_b, bx, ba, lam, hg_g = (cvec_ref[k:k + 1, cs] for k in range(5))
        cw = cw_ref[:, cs]
        xa = [None] * NB
        kk = [None] * NB

        def conv(b):
            def run():
                xc_scr[b, 0:8, :] = tail_scr[c, b]
                xc_scr[b, 8:8 + TT, :] = zcols(Z_XA, b)
                acc = conv_b + cw[3:4] * xc_scr[b, 8:8 + TT, :]
                for k in range(CONV_WIDTH - 1):
                    acc = acc + cw[k:k + 1] * xc_scr[b, 5 + k:5 + k + TT, :]
                xa[b] = acc
                tail_scr[c, b] = xc_scr[b, TT:TT + 8, :]
            return run

        def gate_dot():
            gt_scr[...] = _dot(jnp.concatenate(xa, axis=0).astype(_bf16), wbd_ref[c])

        def gates(b):
            def run():
                neg_lam = -lam
                softplus = jnp.maximum(neg_lam, 0.0) + jnp.log(1.0 + jnp.exp(-jnp.abs(neg_lam)))
                gate_i = _sigmoid(gt_scr[_rows(b), :CB] + bx)
                gate_r = _sigmoid(gt_scr[_rows(b), CB:] + ba)
                a = jnp.exp(((-LRU_C) * softplus) * gate_r)
                mult = _sqrt_nonneg(1.0 - a * a)
                _to_pitched(a_scr, b, a)
                _to_pitched(u_scr, b, mult * gate_i * xa[b])
            return run

        def forget(b):
            def run():
                lbl = lbl_ref[:, cs]
                lmax = jnp.maximum(lbl[0:1], lbl[1:2])
                e0 = jnp.exp(lbl[0:1] - lmax)
                e1 = jnp.exp(lbl[1:2] - lmax)
                lb = e0 / (e0 + e1)
                f = lb + (1.0 - lb) * _sigmoid(zcols(Z_F, b))
                _to_pitched(lf_scr, b, jnp.log(f))
                kk[b] = 1.0 - f
            return run

        cums = [jnp.zeros((NB, LANES), _f32) for _ in range(LT)]
        hs = [None] * LT

        def cumsum_steps(t0):
            def run():
                for t in range(t0, t0 + SCAN_CHUNK):
                    rows = pl.ds(t, NB, stride=PITCH)
                    for l in range(LT):
                        cums[l] = cums[l] + lf_scr[l, rows, :]
                        bc_scr[l, rows, :] = cums[l]
            return run

        def lru_steps(t0):
            def run():
                if t0 == 0:
                    h0 = hcar_scr[c]
                    for l in range(LT):
                        hs[l] = h0[:, l * LANES:(l + 1) * LANES]
                for t in range(t0, t0 + SCAN_CHUNK):
                    rows = pl.ds(t, NB, stride=PITCH)
                    for l in range(LT):
                        hs[l] = a_scr[l, rows, :] * hs[l] + u_scr[l, rows, :]
                        h_scr[l, rows, :] = hs[l]
                if t0 + SCAN_CHUNK == TT:
                    hcar_scr[c] = jnp.concatenate(hs, axis=1)
            return run

        def gated_a(b):
            def run():
                ya_scr[_rows(b), cs] = (_from_pitched(h_scr, b) * _silu(zcols(Z_GA, b))).astype(_bf16)
            return run

        att_in = [None] * NB
        att_w = {}
        o_raw = {}
        heads = [(b, hh) for b in range(NB) for hh in range(HPB)]

        def head_cols(hh):
            return slice(hh * HEAD_DIM, (hh + 1) * HEAD_DIM)

        def att_prologue(b):
            def run():
                bc = _from_pitched(bc_scr, b)
                b_mid = bc[CHUNK // 2:CHUNK // 2 + 1]
                b_last = bc[CHUNK - 1:CHUNK]
                q_in = _silu(zcols(Z_Q, b)) * jnp.exp(bc - b_mid) * HG_SCALE
                k_in = kk[b] * jnp.exp(b_mid - bc)
                q_inter = (q_in * jnp.exp(b_mid)).astype(_bf16)
                k_state = (k_in * jnp.exp(b_last - b_mid)).astype(_bf16)
                att_in[b] = (q_in.astype(_bf16), k_in.astype(_bf16), q_inter, k_state,
                             zcols(Z_I, b).astype(_bf16), jnp.exp(b_last))
            return run

        def att_scores(b, hh):
            def run():
                q_in, k_in = att_in[b][0], att_in[b][1]
                ls = head_cols(hh)
                att_w[b, hh] = jnp.where(causal, _dot_nt(q_in[:, ls], k_in[:, ls]), 0.0).astype(_bf16)
            return run

        def att_output(b, hh):
            def run():
                q_inter, v = att_in[b][2], att_in[b][4]
                ls = head_cols(hh)
                st = st_scr[c, b, hh]
                o_raw[b, hh] = _dot(att_w[b, hh], v[:, ls]) + _dot_nt(q_inter[:, ls], st.astype(_bf16))
            return run

        def state_update(b, hh):
            def run():
                k_state, v, decay = att_in[b][3], att_in[b][4], att_in[b][5]
                ls = head_cols(hh)
                st_scr[c, b, hh] = st_scr[c, b, hh] * decay[:, ls] + _dot_tn(v[:, ls], k_state[:, ls])
            return run

        def gated_b(b):
            def run():
                o_heads = []
                for hh in range(HPB):
                    o_h = o_raw[b, hh]
                    ms = jnp.mean(o_h * o_h, axis=-1, keepdims=True)
                    o_heads.append(o_h * lax.rsqrt(ms + EPS))
                o_n = jnp.concatenate(o_heads, axis=1) * hg_g
                yb_scr[_rows(b), cs] = (o_n * _silu(zcols(Z_GB, b))).astype(_bf16)
            return run

        steps = range(0, TT, SCAN_CHUNK)
        pieces = [conv(b) for b in range(NB)] + [gate_dot] + [forget(b) for b in range(NB)]
        for b, t0 in zip(range(NB), steps):
            pieces += [gates(b), cumsum_steps(t0)]
        for b, t0 in zip(range(NB), steps):
            pieces += [att_prologue(b), lru_steps(t0)]
        pieces += [att_scores(b, hh) for b, hh in heads]
        pieces += [gated_a(b) for b in range(NB // 2)]
        pieces += [att_output(b, hh) for b, hh in heads]
        pieces += [gated_a(b) for b in range(NB // 2, NB)]
        pieces += [state_update(b, hh) for b, hh in heads]
        pieces += [gated_b(b) for b in range(NB)]
        return pieces

    for piece in project_pieces(0):
        piece()
    for c in range(NCB):
        filler = project_pieces(c + 1) if c + 1 < NCB else []
        if c >= NCB - 2:
            base = 2 * (c - (NCB - 2))
            filler = filler + merge_gate_pieces(base) + merge_gate_pieces(base + 1)
        _emit_interleaved(channel_block_pieces(c), filler)

    for n in range(NCB):
        cs = _cols(n)
        out_a = _dot(ya_scr[...], pa_ref[:, cs])
        out_b = _dot(yb_scr[...], pb_ref[:, cs])
        g_a = _sigmoid(gm_scr[:, _cols(2 * n)] + bm_ref[0:1, cs])
        g_b = _sigmoid(gm_scr[:, _cols(2 * n + 1)] + bm_ref[1:2, cs])
        mix_scr[:, cs] = (g_a * out_a + g_b * out_b).astype(_bf16)

    ssq = [jnp.zeros((TT, 1), _f32) for _ in range(NB)]
    for n in range(NCB):
        cs = _cols(n)
        proj = _dot(mix_scr[...], wo_ref[:, cs])
        for b in range(NB):
            res = x_ref[b, :, cs] + proj[_rows(b)]
            o_ref[b, :, cs] = res
            ssq[b] = ssq[b] + jnp.sum(res * res, axis=-1, keepdims=True)
    fg = fg_ref[...]
    for b in range(NB):
        o_ref[b] = o_ref[b] * lax.rsqrt(ssq[b] * (1.0 / D_MODEL) + EPS) * fg


def _resident(shape):
    zeros = (0,) * len(shape)
    return pl.BlockSpec(shape, lambda i, j: zeros, pipeline_mode=pl.Buffered(1))


@jax.jit
def kernel(x, w_in, b_merge, conv_w, conv_b, rg_wx, rg_bx, rg_wa, rg_ba, rg_lambda,
           hg_lb_logits, hg_norm_g, proj_a, proj_b, w_out, norm_g, final_norm_g):
    bsz, seq, d = x.shape
    assert d == D_MODEL and bsz % NB == 0 and seq % TT == 0
    assert w_in.shape[0] == 1, "single-layer block"

    lpb = CB // LRU_BLOCK_DIM
    wbd = jnp.zeros((NCB, CB, 2 * CB), _bf16)
    for c in range(NCB):
        for k in range(lpb):
            r = slice(k * LRU_BLOCK_DIM, (k + 1) * LRU_BLOCK_DIM)
            wbd = wbd.at[c, r, r].set(rg_wx[0, c * lpb + k].astype(_bf16))
            wbd = wbd.at[c, r, CB + k * LRU_BLOCK_DIM:CB + (k + 1) * LRU_BLOCK_DIM].set(
                rg_wa[0, c * lpb + k].astype(_bf16))

    zeros = jnp.zeros((D_MODEL,), _f32)
    cvec = jnp.stack([conv_b[0], rg_bx[0].reshape(-1), rg_ba[0].reshape(-1), rg_lambda[0],
                      jnp.tile(hg_norm_g[0], D_MODEL // HEAD_DIM), zeros, zeros, zeros])
    cw = conv_w[0]
    lbl = hg_lb_logits
    bm = b_merge[0].reshape(2, D_MODEL)
    ng = norm_g[0].reshape(1, D_MODEL)
    fg = final_norm_g.reshape(1, D_MODEL)

    x_spec = pl.BlockSpec((NB, TT, D_MODEL), lambda i, j: (i, j, 0))
    streamed = (w_in, proj_a, proj_b, w_out)
    operands = (wbd, cw, cvec, lbl, bm, ng, fg)
    pitched = pltpu.VMEM((LT, NB * PITCH, LANES), _f32)
    return pl.pallas_call(
        _block_kernel,
        grid=(bsz // NB, seq // TT),
        in_specs=([x_spec] + [pl.BlockSpec(memory_space=pl.ANY) for _ in streamed]
                  + [_resident(op.shape) for op in operands]),
        out_specs=x_spec,
        out_shape=jax.ShapeDtypeStruct(x.shape, x.dtype),
        scratch_shapes=[
            pltpu.VMEM(w_in.shape[1:], _bf16),
            pltpu.VMEM((D_MODEL, D_MODEL), _bf16),
            pltpu.VMEM((D_MODEL, D_MODEL), _bf16),
            pltpu.VMEM((D_MODEL, D_MODEL), _bf16),
            pltpu.VMEM((2, WROWS, w_in.shape[2]), _f32),
            pltpu.VMEM((2, 3, WROWS, D_MODEL), _f32),
            pltpu.SemaphoreType.DMA((2, 4)),
            pltpu.VMEM((M, D_MODEL), _bf16),
            pltpu.VMEM((M, NZ * CB), _f32),
            pltpu.VMEM((M, NZ * CB), _f32),
            pltpu.VMEM((M, 2 * D_MODEL), _f32),
            pltpu.VMEM((M, 2 * CB), _f32),
            pltpu.VMEM((NB, PITCH, CB), _f32),
            pltpu.VMEM((NCB, NB, 8, CB), _f32),
            pitched, pitched, pitched, pitched, pitched,
            pltpu.VMEM((NCB, NB, CB), _f32),
            pltpu.VMEM((NCB, NB, HPB, HEAD_DIM, HEAD_DIM), _f32),
            pltpu.VMEM((M, D_MODEL), _bf16),
            pltpu.VMEM((M, D_MODEL), _bf16),
            pltpu.VMEM((M, D_MODEL), _bf16),
        ],
        compiler_params=pltpu.CompilerParams(
            dimension_semantics=("arbitrary", "arbitrary"),
            vmem_limit_bytes=VMEM_LIMIT_BYTES),
        name="hawk_hgrn2_block",
    )(x, *streamed, *operands)
```
